```python
import math
import jax
import jax.numpy as jnp
from jax import lax
import numpy as np

D_MODEL = 4096
BATCH = 1
SEQ = 16384
DEPTH = 4

N_MIXERS = 3
HEAD_DIM = 128
BLOCK_Q = 128
RMS_EPS = 1e-6

DIL_CONFIGS = ((128, 1), (512, 4), (2048, 16))
N_DIL_GROUPS = len(DIL_CONFIGS)
DIL_HEADS = 16
DIL_SPAN = DIL_CONFIGS[0][0] // DIL_CONFIGS[0][1]

NUM_BUCKETS = 32
MAX_DISTANCE = 2048

FOX_HEADS = 32

MLA_HEADS = 64
MLA_Q_RANK = 1536
MLA_KV_RANK = 512
MLA_NOPE_DIM = 128
MLA_ROPE_DIM = 64
MLA_V_DIM = 128
MLA_QK_DIM = MLA_NOPE_DIM + MLA_ROPE_DIM
ROPE_THETA = 10000.0

D_FF = 11008
CONV_WIDTH = 3

kernel_name = "hybrid_dilated_fox_mla_convffn_trunk"


def rms_norm(x, gain):
    x32 = x.astype(jnp.float32)
    y = x32 * lax.rsqrt(jnp.mean(x32 * x32, axis=-1, keepdims=True) + RMS_EPS)
    return (y * gain.astype(jnp.float32)).astype(x.dtype)


def t5_causal_bucket(dist):
    max_exact = NUM_BUCKETS // 2
    d32 = jnp.maximum(dist, 1).astype(jnp.float32)
    large = max_exact + (jnp.log(d32 / max_exact) / math.log(MAX_DISTANCE / max_exact)
                         * (NUM_BUCKETS - max_exact)).astype(jnp.int32)
    large = jnp.minimum(large, NUM_BUCKETS - 1)
    return jnp.where(dist < max_exact, dist, large)


def dilated_window_attention(q, k, v, bias, dilation):
    b, s, h, hd = q.shape
    sub_len = s // dilation
    n_blk = -(-sub_len // BLOCK_Q)
    padded = n_blk * BLOCK_Q
    tail = padded - sub_len

    def to_sub(t):
        return t.reshape(b, sub_len, dilation, h, hd).transpose(0, 2, 1, 3, 4).reshape(b * dilation, sub_len, h, hd)

    qs = jnp.pad(to_sub(q), ((0, 0), (0, tail), (0, 0), (0, 0)))
    ks = jnp.pad(to_sub(k), ((0, 0), (DIL_SPAN, tail), (0, 0), (0, 0)))
    vs = jnp.pad(to_sub(v), ((0, 0), (DIL_SPAN, tail), (0, 0), (0, 0)))
    qb = qs.reshape(b * dilation, n_blk, BLOCK_Q, h, hd)
    band = BLOCK_Q + DIL_SPAN
    key_idx = jnp.arange(n_blk)[:, None] * BLOCK_Q + jnp.arange(band)[None, :]
    kb = ks[:, key_idx]
    vb = vs[:, key_idx]
    scores = jnp.einsum("znqhd,znkhd->znhqk", qb, kb).astype(jnp.float32) * hd ** -0.5
    rel = jnp.arange(BLOCK_Q)[:, None] + DIL_SPAN - jnp.arange(band)[None, :]
    key_pos = key_idx - DIL_SPAN
    valid = ((rel >= 0) & (rel <= DIL_SPAN))[None] & (key_pos >= 0)[:, None, :]
    pos_bias = bias.astype(jnp.float32)[:, jnp.clip(rel, 0, DIL_SPAN)]
    scores = jnp.where(valid[None, :, None], scores + pos_bias[None, None], -jnp.inf)
    lse = jax.nn.logsumexp(scores, axis=-1)
    probs = jnp.exp(scores - lse[..., None])
    out = jnp.einsum("znhqk,znkhd->znqhd", probs.astype(v.dtype), vb)
    out = out.reshape(b * dilation, padded, h, hd)[:, :sub_len]
    lse = lse.transpose(0, 1, 3, 2).reshape(b * dilation, padded, h)[:, :sub_len]

    def from_sub(t):
        t = t.reshape((b, dilation, sub_len) + t.shape[2:])
        t = jnp.moveaxis(t, 1, 2)
        return t.reshape((b, s) + t.shape[3:])

    return from_sub(out), from_sub(lse)


def blocked_causal_attention(q, k, v, scale, cum_log_f=None):
    b, s, h, dq = q.shape
    n_blk = s // BLOCK_Q
    key_pos = jnp.arange(s)
    xs = (jnp.arange(n_blk), jnp.moveaxis(q.reshape(b, n_blk, BLOCK_Q, h, dq), 1, 0))
    if cum_log_f is not None:
        key_decay = jnp.moveaxis(cum_log_f, 1, 2)
        xs = xs + (jnp.moveaxis(cum_log_f.reshape(b, n_blk, BLOCK_Q, h), 1, 0),)

    def one_block(blk):
        i, q_i = blk[0], blk[1]
        scores = jnp.einsum("bqhd,bkhd->bhqk", q_i, k).astype(jnp.float32) * scale
        if cum_log_f is not None:
            scores = scores + (jnp.moveaxis(blk[2], 1, 2)[..., :, None] - key_decay[:, :, None, :])
        q_pos = i * BLOCK_Q + jnp.arange(BLOCK_Q)
        causal = key_pos[None, :] <= q_pos[:, None]
        probs = jax.nn.softmax(jnp.where(causal, scores, -jnp.inf), axis=-1)
        return jnp.einsum("bhqk,bkhd->bqhd", probs.astype(v.dtype), v)

    out = lax.map(one_block, xs)
    return jnp.moveaxis(out, 0, 1).reshape(b, s, h, v.shape[-1])


def rope(t, positions):
    half = MLA_ROPE_DIM // 2
    freqs = ROPE_THETA ** (-jnp.arange(half, dtype=jnp.float32) / half)
    ang = positions.astype(jnp.float32)[..., None] * freqs
    cos = jnp.cos(ang)[:, :, None, :]
    sin = jnp.sin(ang)[:, :, None, :]
    t32 = t.astype(jnp.float32)
    t1, t2 = t32[..., :half], t32[..., half:]
    return jnp.concatenate([t1 * cos - t2 * sin, t1 * sin + t2 * cos], axis=-1).astype(t.dtype)


def dilated_mixer(x, rel_bias, norm_g, w_in, q_norm, k_norm, w_out):
    b, s, _ = x.shape
    h = rms_norm(x, norm_g)
    qkv = (h @ w_in).reshape(b, s, N_DIL_GROUPS, 3, DIL_HEADS, HEAD_DIM)
    q = rms_norm(qkv[:, :, :, 0], q_norm)
    k = rms_norm(qkv[:, :, :, 1], k_norm)
    v = qkv[:, :, :, 2]
    outs, lses = [], []
    for g, (window, dilation) in enumerate(DIL_CONFIGS):
        dist = jnp.arange(window // dilation + 1, dtype=jnp.int32) * dilation
        bias = rel_bias[t5_causal_bucket(dist), g * DIL_HEADS:(g + 1) * DIL_HEADS].T
        o, lse = dilated_window_attention(q[:, :, g], k[:, :, g], v[:, :, g], bias, dilation)
        outs.append(o)
        lses.append(lse)
    weights = jax.nn.softmax(jnp.stack(lses), axis=0)
    merged = jnp.sum(weights[..., None].astype(x.dtype) * jnp.stack(outs), axis=0)
    return merged.reshape(b, s, DIL_HEADS * HEAD_DIM) @ w_out


def fox_mixer(x, norm_g, w_in, b_f, q_norm, k_norm, w_out):
    b, s, _ = x.shape
    width = FOX_HEADS * HEAD_DIM
    h = rms_norm(x, norm_g)
    proj = h @ w_in
    q = rms_norm(proj[..., :width].reshape(b, s, FOX_HEADS, HEAD_DIM), q_norm)
    k = rms_norm(proj[..., width:2 * width].reshape(b, s, FOX_HEADS, HEAD_DIM), k_norm)
    v = proj[..., 2 * width:3 * width].reshape(b, s, FOX_HEADS, HEAD_DIM)
    log_f = jax.nn.log_sigmoid((proj[..., 3 * width:] + b_f).astype(jnp.float32))
    cum_log_f = jnp.cumsum(log_f, axis=1)
    o = blocked_causal_attention(q, k, v, HEAD_DIM ** -0.5, cum_log_f)
    return o.reshape(b, s, width) @ w_out


def mla_mixer(x, positions, norm_g, w_in, q_a_norm, kv_a_norm, w_q_b, w_kv_b, q_norm, k_norm, w_out):
    b, s, _ = x.shape
    h = rms_norm(x, norm_g)
    proj = h @ w_in
    q_lat = rms_norm(proj[..., :MLA_Q_RANK], q_a_norm)
    kv_lat = rms_norm(proj[..., MLA_Q_RANK:MLA_Q_RANK + MLA_KV_RANK], kv_a_norm)
    k_rope = proj[..., MLA_Q_RANK + MLA_KV_RANK:]
    q = (q_lat @ w_q_b).reshape(b, s, MLA_HEADS, MLA_QK_DIM)
    kv = (kv_lat @ w_kv_b).reshape(b, s, MLA_HEADS, MLA_NOPE_DIM + MLA_V_DIM)
    k_nope, v = kv[..., :MLA_NOPE_DIM], kv[..., MLA_NOPE_DIM:]
    k = jnp.concatenate([k_nope, jnp.broadcast_to(k_rope[:, :, None, :], (b, s, MLA_HEADS, MLA_ROPE_DIM))], axis=-1)
    q = rms_norm(q, q_norm)
    k = rms_norm(k, k_norm)
    q = jnp.concatenate([q[..., :MLA_NOPE_DIM], rope(q[..., MLA_NOPE_DIM:], positions)], axis=-1)
    k = jnp.concatenate([k[..., :MLA_NOPE_DIM], rope(k[..., MLA_NOPE_DIM:], positions)], axis=-1)
    o = blocked_causal_attention(q, k, v, MLA_QK_DIM ** -0.5)
    return o.reshape(b, s, MLA_HEADS * MLA_V_DIM) @ w_out


def conv_ffn(x, norm_g, w_up, conv_w, conv_b, w_down):
    h = rms_norm(x, norm_g)
    u = h @ w_up
    u = lax.conv_general_dilated(
        u, conv_w[:, None, :].astype(u.dtype), window_strides=(1,),
        padding=((CONV_WIDTH - 1, 0),), dimension_numbers=("NWC", "WIO", "NWC"),
        feature_group_count=u.shape[-1]) + conv_b
    gate, val = u[..., :D_FF], u[..., D_FF:]
    return (jax.nn.silu(gate) * val) @ w_down


def setup_inputs(seed: int = 0) -> dict:
    key = jax.random.key(seed)
    keys = iter(jax.random.split(key, 128))

    def normal(shape, scale):
        return scale * jax.random.normal(next(keys), shape, jnp.float32)

    def dense(fan_in, fan_out):
        return normal((fan_in, fan_out), fan_in ** -0.5)

    def gain(n):
        return 1.0 + normal((n,), 0.02)

    inputs = {
        "x": normal((BATCH, SEQ, D_MODEL), 1.0),
        "positions": jnp.broadcast_to(jnp.arange(SEQ, dtype=jnp.int32), (BATCH, SEQ)),
        "rel_bias": normal((NUM_BUCKETS, N_DIL_GROUPS * DIL_HEADS), 0.1),
    }
    for i in range(DEPTH):
        p = f"l{i}_"
        kind = i % N_MIXERS
        inputs[p + "attn_norm"] = gain(D_MODEL)
        if kind == 0:
            inputs[p + "w_in"] = dense(D_MODEL, N_DIL_GROUPS * 3 * DIL_HEADS * HEAD_DIM)
            inputs[p + "q_norm"] = gain(HEAD_DIM)
            inputs[p + "k_norm"] = gain(HEAD_DIM)
            inputs[p + "w_out"] = dense(DIL_HEADS * HEAD_DIM, D_MODEL)
        elif kind == 1:
            inputs[p + "w_in"] = dense(D_MODEL, 3 * FOX_HEADS * HEAD_DIM + FOX_HEADS)
            inputs[p + "b_f"] = normal((FOX_HEADS,), 0.1)
            inputs[p + "q_norm"] = gain(HEAD_DIM)
            inputs[p + "k_norm"] = gain(HEAD_DIM)
            inputs[p + "w_out"] = dense(FOX_HEADS * HEAD_DIM, D_MODEL)
        else:
            inputs[p + "w_in"] = dense(D_MODEL, MLA_Q_RANK + MLA_KV_RANK + MLA_ROPE_DIM)
            inputs[p + "q_a_norm"] = gain(MLA_Q_RANK)
            inputs[p + "kv_a_norm"] = gain(MLA_KV_RANK)
            inputs[p + "w_q_b"] = dense(MLA_Q_RANK, MLA_HEADS * MLA_QK_DIM)
            inputs[p + "w_kv_b"] = dense(MLA_KV_RANK, MLA_HEADS * (MLA_NOPE_DIM + MLA_V_DIM))
            inputs[p + "q_norm"] = gain(MLA_QK_DIM)
            inputs[p + "k_norm"] = gain(MLA_QK_DIM)
            inputs[p + "w_out"] = dense(MLA_HEADS * MLA_V_DIM, D_MODEL)
        inputs[p + "ffn_norm"] = gain(D_MODEL)
        inputs[p + "w_up"] = dense(D_MODEL, 2 * D_FF)
        inputs[p + "conv_w"] = normal((CONV_WIDTH, 2 * D_FF), CONV_WIDTH ** -0.5)
        inputs[p + "conv_b"] = normal((2 * D_FF,), 0.02)
        inputs[p + "w_down"] = dense(D_FF, D_MODEL)
    return inputs


def reference(x, positions, rel_bias,
              l0_attn_norm, l0_w_in, l0_q_norm, l0_k_norm, l0_w_out,
              l0_ffn_norm, l0_w_up, l0_conv_w, l0_conv_b, l0_w_down,
              l1_attn_norm, l1_w_in, l1_b_f, l1_q_norm, l1_k_norm, l1_w_out,
              l1_ffn_norm, l1_w_up, l1_conv_w, l1_conv_b, l1_w_down,
              l2_attn_norm, l2_w_in, l2_q_a_norm, l2_kv_a_norm, l2_w_q_b, l2_w_kv_b, l2_q_norm, l2_k_norm, l2_w_out,
              l2_ffn_norm, l2_w_up, l2_conv_w, l2_conv_b, l2_w_down,
              l3_attn_norm, l3_w_in, l3_q_norm, l3_k_norm, l3_w_out,
              l3_ffn_norm, l3_w_up, l3_conv_w, l3_conv_b, l3_w_down):
    mixers = (dilated_mixer, fox_mixer, mla_mixer)
    mixer_args = (
        (rel_bias, l0_attn_norm, l0_w_in, l0_q_norm, l0_k_norm, l0_w_out),
        (l1_attn_norm, l1_w_in, l1_b_f, l1_q_norm, l1_k_norm, l1_w_out),
        (positions, l2_attn_norm, l2_w_in, l2_q_a_norm, l2_kv_a_norm, l2_w_q_b, l2_w_kv_b, l2_q_norm, l2_k_norm, l2_w_out),
        (rel_bias, l3_attn_norm, l3_w_in, l3_q_norm, l3_k_norm, l3_w_out),
    )
    ffn_args = (
        (l0_ffn_norm, l0_w_up, l0_conv_w, l0_conv_b, l0_w_down),
        (l1_ffn_norm, l1_w_up, l1_conv_w, l1_conv_b, l1_w_down),
        (l2_ffn_norm, l2_w_up, l2_conv_w, l2_conv_b, l2_w_down),
        (l3_ffn_norm, l3_w_up, l3_conv_w, l3_conv_b, l3_w_down),
    )
    for i in range(DEPTH):
        x = x + mixers[i % N_MIXERS](x, *mixer_args[i])
        x = x + conv_ffn(x, *ffn_args[i])
    return x
```

```python
import functools
import math

import jax
import jax.numpy as jnp
from jax import lax
from jax.experimental import pallas as pl
from jax.experimental.pallas import tpu as pltpu

F32 = jnp.float32
BF16 = jnp.bfloat16

V7X_LANES = 128
V7X_VMEM_BYTES = 64 * 1024 * 1024
VMEM_LIMIT_BYTES = V7X_VMEM_BYTES - 8 * 1024 * 1024

HEAD_DIM = 128
RMS_EPS = 1e-6
DIL_CONFIGS = ((128, 1), (512, 4), (2048, 16))
DIL_HEADS = 16
DIL_SPAN = 128
NUM_BUCKETS = 32
MAX_DISTANCE = 2048
FOX_HEADS = 32
MLA_HEADS = 64
MLA_Q_RANK = 1536
MLA_KV_RANK = 512
MLA_NOPE_DIM = 128
MLA_ROPE_DIM = 64
MLA_V_DIM = 128
MLA_QK_DIM = MLA_NOPE_DIM + MLA_ROPE_DIM
MLA_QK_PAD = 2 * V7X_LANES
ROPE_THETA = 10000.0
D_FF = 11008
MASK_VALUE = -1e30


def _cparams(n_axes):
    return pltpu.CompilerParams(dimension_semantics=("arbitrary",) * n_axes,
                                vmem_limit_bytes=VMEM_LIMIT_BYTES)


def _tile(dim, pref):
    t = min(dim, pref)
    while dim % t:
        t //= 2
    return t


def _rmsnorm_kernel(x_ref, g_ref, o_ref):
    x = x_ref[...].astype(F32)
    ms = jnp.mean(x * x, axis=-1, keepdims=True)
    o_ref[...] = (x * lax.rsqrt(ms + RMS_EPS) * g_ref[...]).astype(o_ref.dtype)


def rmsnorm(x, gain, *, width=None, col_block=0, rows=256):
    s, wtot = x.shape
    width = wtot if width is None else width
    tr = _tile(s, rows)
    return pl.pallas_call(
        _rmsnorm_kernel,
        grid=(s // tr,),
        in_specs=[pl.BlockSpec((tr, width), lambda i: (i, col_block)),
                  pl.BlockSpec((1, width), lambda i: (0, 0))],
        out_specs=pl.BlockSpec((tr, width), lambda i: (i, 0)),
        out_shape=jax.ShapeDtypeStruct((s, width), BF16),
        compiler_params=_cparams(1),
        name="rmsnorm",
    )(x, gain.reshape(1, width).astype(F32))


def _mm_kernel(*refs, nk, has_res):
    if has_res:
        a_ref, w_ref, r_ref, o_ref = refs[:4]
        scratch = refs[4:]
    else:
        a_ref, w_ref, o_ref = refs[:3]
        r_ref = None
        scratch = refs[3:]

    def finish(acc):
        if has_res:
            acc = acc + r_ref[...]
        o_ref[...] = acc.astype(o_ref.dtype)

    if nk == 1:
        finish(jnp.dot(a_ref[...], w_ref[...], preferred_element_type=F32))
        return

    acc_ref = scratch[0]
    k = pl.program_id(2)

    @pl.when(k == 0)
    def _():
        acc_ref[...] = jnp.zeros_like(acc_ref)

    acc_ref[...] += jnp.dot(a_ref[...], w_ref[...], preferred_element_type=F32)

    @pl.when(k == nk - 1)
    def _():
        finish(acc_ref[...])


def matmul(a, w, *, residual=None, out_dtype=BF16, tm=1024, tn=1024, tk=4096):
    m, kdim = a.shape
    _, n = w.shape
    tm, tn, tk = _tile(m, tm), _tile(n, tn), _tile(kdim, tk)
    nk = kdim // tk
    in_specs = [pl.BlockSpec((tm, tk), lambda i, j, k: (i, k)),
                pl.BlockSpec((tk, tn), lambda i, j, k: (k, j))]
    args = [a, w]
    if residual is not None:
        in_specs.append(pl.BlockSpec((tm, tn), lambda i, j, k: (i, j)))
        args.append(residual)
    return pl.pallas_call(
        functools.partial(_mm_kernel, nk=nk, has_res=residual is not None),
        grid=(m // tm, n // tn, nk),
        in_specs=in_specs,
        out_specs=pl.BlockSpec((tm, tn), lambda i, j, k: (i, j)),
        out_shape=jax.ShapeDtypeStruct((m, n), out_dtype),
        scratch_shapes=[pltpu.VMEM((tm, tn), F32)] if nk > 1 else [],
        compiler_params=_cparams(3),
        name="matmul",
    )(*args)


def _t5_causal_bucket(dist):
    max_exact = NUM_BUCKETS // 2
    d32 = jnp.maximum(dist, 1).astype(F32)
    large = max_exact + (jnp.log(d32 / max_exact) / math.log(MAX_DISTANCE / max_exact)
                         * (NUM_BUCKETS - max_exact)).astype(jnp.int32)
    large = jnp.minimum(large, NUM_BUCKETS - 1)
    return jnp.where(dist < max_exact, dist, large)


def _dil_bucket_onehot():
    band = BLOCK_Q_DIL + DIL_SPAN
    rel = jnp.arange(BLOCK_Q_DIL)[:, None] + DIL_SPAN - jnp.arange(band)[None, :]
    valid = ((rel >= 0) & (rel <= DIL_SPAN)).astype(F32)
    relc = jnp.clip(rel, 0, DIL_SPAN)
    hots = []
    for window, dilation in DIL_CONFIGS:
        dist = jnp.arange(window // dilation + 1, dtype=jnp.int32) * dilation
        bucket = _t5_causal_bucket(dist)[relc]
        hots.append((bucket[None] == jnp.arange(NUM_BUCKETS)[:, None, None]).astype(F32))
    return jnp.stack(hots), valid


BLOCK_Q_DIL = 128


def _dil_bias_kernel(rb_ref, oh_ref, valid_ref, o_ref):
    g = pl.program_id(0)
    valid = valid_ref[...] > 0
    for h in range(DIL_HEADS):
        def body(b, acc, h=h):
            return acc + rb_ref[b, g * DIL_HEADS + h] * oh_ref[0, b]
        acc = lax.fori_loop(0, NUM_BUCKETS, body, jnp.zeros(valid_ref.shape, F32))
        o_ref[0, h] = jnp.where(valid, acc, -jnp.inf)


def dil_bias_tiles(rel_bias):
    onehot, valid = _dil_bucket_onehot()
    ng = len(DIL_CONFIGS)
    band = BLOCK_Q_DIL + DIL_SPAN
    return pl.pallas_call(
        _dil_bias_kernel,
        grid=(ng,),
        in_specs=[pl.BlockSpec(memory_space=pltpu.SMEM),
                  pl.BlockSpec((1, NUM_BUCKETS, BLOCK_Q_DIL, band), lambda g: (g, 0, 0, 0)),
                  pl.BlockSpec((BLOCK_Q_DIL, band), lambda g: (0, 0))],
        out_specs=pl.BlockSpec((1, DIL_HEADS, BLOCK_Q_DIL, band), lambda g: (g, 0, 0, 0)),
        out_shape=jax.ShapeDtypeStruct((ng, DIL_HEADS, BLOCK_Q_DIL, band), F32),
        compiler_params=_cparams(1),
        name="dil_bias",
    )(rel_bias.astype(F32), onehot, valid)


def _dil_attn_kernel(q_ref, kp_ref, kc_ref, vp_ref, vc_ref, bias_ref, qg_ref, kg_ref, o_ref, l_ref):
    has_prev = pl.program_id(1) > 0
    tq = q_ref.shape[0]
    scale = HEAD_DIM ** -0.5
    qg = qg_ref[...] * scale
    kg = kg_ref[...]
    col = lax.broadcasted_iota(jnp.int32, (tq, 2 * tq), 1)
    keep = (col >= tq) | has_prev
    for h in range(DIL_HEADS):
        sl = slice(h * HEAD_DIM, (h + 1) * HEAD_DIM)
        q = q_ref[:, sl].astype(F32)
        qn = q * lax.rsqrt(jnp.mean(q * q, axis=-1, keepdims=True) + RMS_EPS) * qg
        k = jnp.concatenate([kp_ref[:, sl], kc_ref[:, sl]], axis=0).astype(F32)
        kn = k * lax.rsqrt(jnp.mean(k * k, axis=-1, keepdims=True) + RMS_EPS) * kg
        s = lax.dot_general(qn.astype(BF16), kn.astype(BF16), (((1,), (1,)), ((), ())),
                            preferred_element_type=F32)
        s = jnp.where(keep, s + bias_ref[h], -jnp.inf)
        m = jnp.max(s, axis=-1, keepdims=True)
        p = jnp.exp(s - m)
        l = jnp.sum(p, axis=-1, keepdims=True)
        v = jnp.concatenate([vp_ref[:, sl], vc_ref[:, sl]], axis=0)
        o = jnp.dot(p.astype(BF16), v, preferred_element_type=F32) / l
        o_ref[:, sl] = o.astype(o_ref.dtype)
        l_ref[:, sl] = jnp.broadcast_to(m + jnp.log(l), (tq, HEAD_DIM))


def dil_attention_group(qkv, bias_g, q_gain, k_gain, g, dilation):
    s, wtot = qkv.shape
    hw = DIL_HEADS * HEAD_DIM
    nblk_per_row = wtot // hw
    sub_len = s // dilation
    tq = BLOCK_Q_DIL
    view = qkv.reshape(sub_len, dilation * wtot)
    cq, ck, cv = g * 3, g * 3 + 1, g * 3 + 2

    def cur(c):
        return lambda r, i: (i, r * nblk_per_row + c)

    def prev(c):
        return lambda r, i: (jnp.maximum(i - 1, 0), r * nblk_per_row + c)

    blk = (tq, hw)
    o, lse = pl.pallas_call(
        _dil_attn_kernel,
        grid=(dilation, sub_len // tq),
        in_specs=[pl.BlockSpec(blk, cur(cq)),
                  pl.BlockSpec(blk, prev(ck)), pl.BlockSpec(blk, cur(ck)),
                  pl.BlockSpec(blk, prev(cv)), pl.BlockSpec(blk, cur(cv)),
                  pl.BlockSpec((DIL_HEADS, tq, 2 * tq), lambda r, i: (0, 0, 0)),
                  pl.BlockSpec((1, HEAD_DIM), lambda r, i: (0, 0)),
                  pl.BlockSpec((1, HEAD_DIM), lambda r, i: (0, 0))],
        out_specs=[pl.BlockSpec(blk, lambda r, i: (i, r)),
                   pl.BlockSpec(blk, lambda r, i: (i, r))],
        out_shape=[jax.ShapeDtypeStruct((sub_len, dilation * hw), F32),
                   jax.ShapeDtypeStruct((sub_len, dilation * hw), F32)],
        compiler_params=_cparams(2),
        name="dil_attn",
    )(view, view, view, view, view, bias_g,
      q_gain.reshape(1, HEAD_DIM).astype(F32), k_gain.reshape(1, HEAD_DIM).astype(F32))
    return o.reshape(s, hw), lse.reshape(s, hw)


def _dil_merge_kernel(o0, o1, o2, l0, l1, l2, out_ref):
    a, b, c = l0[...], l1[...], l2[...]
    m = jnp.maximum(jnp.maximum(a, b), c)
    ea, eb, ec = jnp.exp(a - m), jnp.exp(b - m), jnp.exp(c - m)
    den = ea + eb + ec
    out = (ea / den) * o0[...] + (eb / den) * o1[...] + (ec / den) * o2[...]
    out_ref[...] = out.astype(out_ref.dtype)


def dil_merge(outs, lses):
    s, hw = outs[0].shape
    tr = _tile(s, 256)
    spec = pl.BlockSpec((tr, hw), lambda i: (i, 0))
    return pl.pallas_call(
        _dil_merge_kernel,
        grid=(s // tr,),
        in_specs=[spec] * 6,
        out_specs=spec,
        out_shape=jax.ShapeDtypeStruct((s, hw), BF16),
        compiler_params=_cparams(1),
        name="dil_merge",
    )(*outs, *lses)


def dilated_mixer(x, bias_tiles, norm_g, w_in, q_norm, k_norm, w_out):
    h = rmsnorm(x, norm_g)
    qkv = matmul(h, w_in.astype(BF16))
    outs, lses = [], []
    for g, (_, dilation) in enumerate(DIL_CONFIGS):
        o, lse = dil_attention_group(qkv, bias_tiles[g], q_norm, k_norm, g, dilation)
        outs.append(o)
        lses.append(lse)
    merged = dil_merge(outs, lses)
    return matmul(merged, w_out.astype(BF16), residual=x, out_dtype=F32)


def _head_prep_kernel(*refs, npieces, dn, scale, rope):
    xs = [refs[p][...].astype(F32) for p in range(npieces)]
    g_ref = refs[npieces]
    nxt = npieces + 1
    if rope:
        c_ref, s1_ref, s2_ref = refs[nxt:nxt + 3]
        nxt += 3
    o_ref = refs[nxt]
    ssq = xs[0] * xs[0]
    ssq = jnp.sum(ssq, axis=-1, keepdims=True)
    for x in xs[1:]:
        ssq = ssq + jnp.sum(x * x, axis=-1, keepdims=True)
    r = lax.rsqrt(ssq * (1.0 / dn) + RMS_EPS) * scale
    for p, x in enumerate(xs):
        sl = slice(p * V7X_LANES, (p + 1) * V7X_LANES)
        y = x * r * g_ref[:, sl]
        if rope and p == npieces - 1:
            y = (y * c_ref[...] + pltpu.roll(y, 96, 1) * s1_ref[...]
                 + pltpu.roll(y, 32, 1) * s2_ref[...])
        o_ref[:, sl] = y.astype(o_ref.dtype)


def head_prep(pieces, gain, n_heads, *, dn, scale=1.0, rope_tables=None, rows=1024):
    s = pieces[0][0].shape[0]
    tr = _tile(s, rows)
    npieces = len(pieces)
    w = npieces * V7X_LANES
    in_specs = [pl.BlockSpec((tr, V7X_LANES), (lambda i, h, f=f: (i, f(h)))) for _, f in pieces]
    args = [a for a, _ in pieces]
    in_specs.append(pl.BlockSpec((1, w), lambda i, h: (0, 0)))
    args.append(gain.reshape(1, w).astype(F32))
    if rope_tables is not None:
        for t in rope_tables:
            in_specs.append(pl.BlockSpec((tr, V7X_LANES), lambda i, h: (i, 0)))
            args.append(t)
    return pl.pallas_call(
        functools.partial(_head_prep_kernel, npieces=npieces, dn=dn, scale=scale,
                          rope=rope_tables is not None),
        grid=(s // tr, n_heads),
        in_specs=in_specs,
        out_specs=pl.BlockSpec((tr, w), lambda i, h: (i, h)),
        out_shape=jax.ShapeDtypeStruct((s, n_heads * w), BF16),
        compiler_params=_cparams(2),
        name="head_prep",
    )(*args)


def _flash_kernel(*refs, tq, tk, has_bias):
    if has_bias:
        q_ref, k_ref, v_ref, kb_ref, o_ref, m_ref, l_ref, acc_ref = refs
    else:
        q_ref, k_ref, v_ref, o_ref, m_ref, l_ref, acc_ref = refs
        kb_ref = None
    i = pl.program_id(1)
    ratio = tq // tk
    q = q_ref[...]
    m_ref[...] = jnp.full(m_ref.shape, MASK_VALUE, F32)
    l_ref[...] = jnp.zeros(l_ref.shape, F32)
    acc_ref[...] = jnp.zeros(acc_ref.shape, F32)
    if has_bias:
        anchor = kb_ref[0, :, pl.ds(pl.multiple_of(i * tq, tq), V7X_LANES)][:, 0:1]

    def step(j, diag_offset):
        start = pl.multiple_of(j * tk, tk)
        k = k_ref[pl.ds(start, tk), :]
        v = v_ref[pl.ds(start, tk), :]
        s = lax.dot_general(q, k, (((1,), (1,)), ((), ())), preferred_element_type=F32)
        if has_bias:
            s = s + (anchor - kb_ref[0, :, pl.ds(start, tk)])
        if diag_offset is not None:
            row = lax.broadcasted_iota(jnp.int32, (tq, tk), 0)
            colg = lax.broadcasted_iota(jnp.int32, (tq, tk), 1) + diag_offset
            s = jnp.where(row >= colg, s, MASK_VALUE)
        m_old = m_ref[...]
        m_new = jnp.maximum(m_old, jnp.max(s, axis=-1, keepdims=True))
        alpha = jnp.exp(m_old - m_new)
        p = jnp.exp(s - m_new)
        l_ref[...] = alpha * l_ref[...] + jnp.sum(p, axis=-1, keepdims=True)
        acc_ref[...] = alpha * acc_ref[...] + jnp.dot(p.astype(BF16), v, preferred_element_type=F32)
        m_ref[...] = m_new

    def full_step(j, carry):
        step(j, None)
        return carry

    lax.fori_loop(0, i * ratio, full_step, 0)
    for d in range(ratio):
        step(i * ratio + d, d * tk)
    o_ref[...] = (acc_ref[...] / l_ref[...]).astype(o_ref.dtype)


def flash_attention(q, k, v, n_heads, *, dq, dv, v_col, key_bias=None, tq=512, tk=512):
    s = q.shape[0]
    tq, tk = _tile(s, tq), _tile(s, tk)
    tk = min(tk, tq)
    in_specs = [pl.BlockSpec((tq, dq), lambda h, i: (i, h)),
                pl.BlockSpec((s, dq), lambda h, i: (0, h)),
                pl.BlockSpec((s, dv), lambda h, i: (0, v_col(h)))]
    args = [q, k, v]
    if key_bias is not None:
        in_specs.append(pl.BlockSpec((1, 1, s), lambda h, i: (h, 0, 0)))
        args.append(key_bias.reshape(n_heads, 1, s))
    return pl.pallas_call(
        functools.partial(_flash_kernel, tq=tq, tk=tk, has_bias=key_bias is not None),
        grid=(n_heads, s // tq),
        in_specs=in_specs,
        out_specs=pl.BlockSpec((tq, dv), lambda h, i: (i, h)),
        out_shape=jax.ShapeDtypeStruct((s, n_heads * dv), BF16),
        scratch_shapes=[pltpu.VMEM((tq, 1), F32), pltpu.VMEM((tq, 1), F32), pltpu.VMEM((tq, dv), F32)],
        compiler_params=_cparams(2),
        name="flash_attn",
    )(*args)


def _fox_gate_kernel(h_ref, wg_ref, bf_ref, o_ref, carry_ref):
    i = pl.program_id(0)
    ts = h_ref.shape[0]

    @pl.when(i == 0)
    def _():
        carry_ref[...] = jnp.zeros_like(carry_ref)

    z = lax.dot_general(wg_ref[...], h_ref[...], (((1,), (1,)), ((), ())),
                        preferred_element_type=F32) + bf_ref[...]
    lf = jnp.minimum(z, 0.0) - jnp.log(1.0 + jnp.exp(-jnp.abs(z)))
    row = lax.broadcasted_iota(jnp.int32, (ts, ts), 0)
    colm = lax.broadcasted_iota(jnp.int32, (ts, ts), 1)
    upper = (row <= colm).astype(BF16)
    hi = lf.astype(BF16)
    r1 = lf - hi.astype(F32)
    mid = r1.astype(BF16)
    lo = (r1 - mid.astype(F32)).astype(BF16)
    cs = (jnp.dot(hi, upper, preferred_element_type=F32)
          + jnp.dot(mid, upper, preferred_element_type=F32)
          + jnp.dot(lo, upper, preferred_element_type=F32))
    cs = cs + carry_ref[:, 0:1]
    o_ref[...] = cs
    carry_ref[...] = jnp.broadcast_to(cs[:, ts - 1:ts], carry_ref.shape)


def fox_cum_log_gate(h, w_gate_t, b_f):
    s, d = h.shape
    nh = w_gate_t.shape[0]
    ts = _tile(s, 512)
    return pl.pallas_call(
        _fox_gate_kernel,
        grid=(s // ts,),
        in_specs=[pl.BlockSpec((ts, d), lambda i: (i, 0)),
                  pl.BlockSpec((nh, d), lambda i: (0, 0)),
                  pl.BlockSpec((nh, 1), lambda i: (0, 0))],
        out_specs=pl.BlockSpec((nh, ts), lambda i: (0, i)),
        out_shape=jax.ShapeDtypeStruct((nh, s), F32),
        scratch_shapes=[pltpu.VMEM((nh, V7X_LANES), F32)],
        compiler_params=_cparams(1),
        name="fox_gate",
    )(h, w_gate_t, b_f.reshape(nh, 1).astype(F32))


def fox_mixer(x, norm_g, w_in, b_f, q_norm, k_norm, w_out):
    width = FOX_HEADS * HEAD_DIM
    h = rmsnorm(x, norm_g)
    proj = matmul(h, w_in[:, :3 * width].astype(BF16))
    cum = fox_cum_log_gate(h, w_in[:, 3 * width:].T.astype(BF16), b_f)
    q = head_prep([(proj, lambda hh: hh)], q_norm, FOX_HEADS, dn=HEAD_DIM, scale=HEAD_DIM ** -0.5)
    k = head_prep([(proj, lambda hh: FOX_HEADS + hh)], k_norm, FOX_HEADS, dn=HEAD_DIM)
    o = flash_attention(q, k, proj, FOX_HEADS, dq=HEAD_DIM, dv=HEAD_DIM,
                        v_col=lambda hh: 2 * FOX_HEADS + hh, key_bias=cum)
    return matmul(o, w_out.astype(BF16), residual=x, out_dtype=F32, tk=2048)


def _rope_table_kernel(pos_ref, freq_ref, c_ref, s1_ref, s2_ref):
    ang = pos_ref[...].astype(F32) * freq_ref[...]
    lane = lax.broadcasted_iota(jnp.int32, ang.shape, 1)
    half = MLA_ROPE_DIM // 2
    c = jnp.cos(ang)
    sn = jnp.sin(ang)
    c_ref[...] = jnp.where(lane < 2 * half, c, 0.0)
    s1_ref[...] = jnp.where(lane < half, -sn, 0.0)
    s2_ref[...] = jnp.where((lane >= half) & (lane < 2 * half), sn, 0.0)


def rope_tables(positions):
    s = positions.shape[0]
    half = MLA_ROPE_DIM // 2
    freqs = ROPE_THETA ** (-jnp.arange(half, dtype=F32) / half)
    freq_row = jnp.concatenate([freqs, freqs, jnp.zeros((V7X_LANES - 2 * half,), F32)]).reshape(1, V7X_LANES)
    tr = _tile(s, 1024)
    spec = pl.BlockSpec((tr, V7X_LANES), lambda i: (i, 0))
    return pl.pallas_call(
        _rope_table_kernel,
        grid=(s // tr,),
        in_specs=[pl.BlockSpec((tr, 1), lambda i: (i, 0)), pl.BlockSpec((1, V7X_LANES), lambda i: (0, 0))],
        out_specs=[spec] * 3,
        out_shape=[jax.ShapeDtypeStruct((s, V7X_LANES), F32)] * 3,
        compiler_params=_cparams(1),
        name="rope_tables",
    )(positions.reshape(s, 1), freq_row)


def _pad_cols(w, to):
    return jnp.pad(w, ((0, 0), (0, to - w.shape[1])))


def mla_mixer(x, positions, norm_g, w_in, q_a_norm, kv_a_norm, w_q_b, w_kv_b, q_norm, k_norm, w_out):
    nope, rope_d, pad = MLA_NOPE_DIM, MLA_ROPE_DIM, MLA_QK_PAD
    lat_w = MLA_Q_RANK + MLA_KV_RANK
    in_w = 2304
    h = rmsnorm(x, norm_g)
    proj = matmul(h, _pad_cols(w_in, in_w).astype(BF16), out_dtype=F32, tn=1152)
    q_lat = rmsnorm(proj, q_a_norm, width=MLA_Q_RANK, col_block=0)
    kv_lat = rmsnorm(proj, kv_a_norm, width=MLA_KV_RANK, col_block=MLA_Q_RANK // MLA_KV_RANK)
    wq = jnp.pad(w_q_b.reshape(MLA_Q_RANK, MLA_HEADS, MLA_QK_DIM), ((0, 0), (0, 0), (0, pad - MLA_QK_DIM)))
    q_raw = matmul(q_lat, wq.reshape(MLA_Q_RANK, MLA_HEADS * pad).astype(BF16))
    kv_raw = matmul(kv_lat, w_kv_b.astype(BF16))
    tabs = rope_tables(positions)
    qg = jnp.pad(q_norm, (0, pad - MLA_QK_DIM))
    kg = jnp.pad(k_norm, (0, pad - MLA_QK_DIM))
    q = head_prep([(q_raw, lambda hh: 2 * hh), (q_raw, lambda hh: 2 * hh + 1)], qg, MLA_HEADS,
                  dn=MLA_QK_DIM, scale=MLA_QK_DIM ** -0.5, rope_tables=tabs)
    k = head_prep([(kv_raw, lambda hh: 2 * hh), (proj, lambda hh: lat_w // V7X_LANES)], kg, MLA_HEADS,
                  dn=MLA_QK_DIM, rope_tables=tabs)
    o = flash_attention(q, k, kv_raw, MLA_HEADS, dq=pad, dv=MLA_V_DIM, v_col=lambda hh: 2 * hh + 1)
    return matmul(o, w_out.astype(BF16), residual=x, out_dtype=F32, tk=2048)


def _ffn_up_kernel(a_ref, wg_ref, wv_ref, cwg_ref, cwv_ref, cbg_ref, cbv_ref, o_ref, carry_ref):
    i = pl.program_id(0)
    j = pl.program_id(1)
    tm = a_ref.shape[0]

    @pl.when(i == 0)
    def _():
        carry_ref[j] = jnp.zeros(carry_ref.shape[1:], F32)

    a = a_ref[...]
    row = lax.broadcasted_iota(jnp.int32, (tm, o_ref.shape[1]), 0)

    def conv(u, cw_ref, cb_ref, slot):
        prev = carry_ref[j, slot]
        carry_ref[j, slot] = u[tm - 8:, :]
        u1 = jnp.where(row == 0, prev[7:8, :], pltpu.roll(u, 1, 0))
        u2 = jnp.where(row == 0, prev[6:7, :], jnp.where(row == 1, prev[7:8, :], pltpu.roll(u, 2, 0)))
        return cw_ref[2:3, :] * u + cw_ref[1:2, :] * u1 + cw_ref[0:1, :] * u2 + cb_ref[...]

    gate = conv(jnp.dot(a, wg_ref[...], preferred_element_type=F32), cwg_ref, cbg_ref, 0)
    val = conv(jnp.dot(a, wv_ref[...], preferred_element_type=F32), cwv_ref, cbv_ref, 1)
    o_ref[...] = (gate * (1.0 / (1.0 + jnp.exp(-gate))) * val).astype(o_ref.dtype)


def ffn_up(h, w_up, conv_w, conv_b, *, tm=1024, tn=256):
    s, d = h.shape
    f = w_up.shape[1] // 2
    tm = _tile(s, tm)
    nj = f // tn
    cw = conv_w.astype(F32)
    cb = conv_b.reshape(1, 2 * f).astype(F32)
    return pl.pallas_call(
        _ffn_up_kernel,
        grid=(s // tm, nj),
        in_specs=[pl.BlockSpec((tm, d), lambda i, j: (i, 0)),
                  pl.BlockSpec((d, tn), lambda i, j: (0, j)),
                  pl.BlockSpec((d, tn), lambda i, j: (0, j + nj)),
                  pl.BlockSpec((3, tn), lambda i, j: (0, j)),
                  pl.BlockSpec((3, tn), lambda i, j: (0, j + nj)),
                  pl.BlockSpec((1, tn), lambda i, j: (0, j)),
                  pl.BlockSpec((1, tn), lambda i, j: (0, j + nj))],
        out_specs=pl.BlockSpec((tm, tn), lambda i, j: (i, j)),
        out_shape=jax.ShapeDtypeStruct((s, f), BF16),
        scratch_shapes=[pltpu.VMEM((nj, 2, 8, tn), F32)],
        compiler_params=_cparams(2),
        name="ffn_up",
    )(h, w_up, w_up, cw, cw, cb, cb)


def conv_ffn(x, norm_g, w_up, conv_w, conv_b, w_down):
    h = rmsnorm(x, norm_g)
    y = ffn_up(h, w_up.astype(BF16), conv_w, conv_b)
    return matmul(y, w_down.astype(BF16), residual=x, out_dtype=F32, tn=512, tk=w_down.shape[0] // 2)


def kernel(x, positions, rel_bias,
           l0_attn_norm, l0_w_in, l0_q_norm, l0_k_norm, l0_w_out,
           l0_ffn_norm, l0_w_up, l0_conv_w, l0_conv_b, l0_w_down,
           l1_attn_norm, l1_w_in, l1_b_f, l1_q_norm, l1_k_norm, l1_w_out,
           l1_ffn_norm, l1_w_up, l1_conv_w, l1_conv_b, l1_w_down,
           l2_attn_norm, l2_w_in, l2_q_a_norm, l2_kv_a_norm, l2_w_q_b, l2_w_kv_b, l2_q_norm, l2_k_norm, l2_w_out,
           l2_ffn_norm, l2_w_up, l2_conv_w, l2_conv_b, l2_w_down,
           l3_attn_norm, l3_w_in, l3_q_norm, l3_k_norm, l3_w_out,
           l3_ffn_norm, l3_w_up, l3_conv_w, l3_conv_b, l3_w_down):
    b, s, d = x.shape
    outs = []
    bias_tiles = dil_bias_tiles(rel_bias)
    for bi in range(b):
        xs = x[bi]
        xs = dilated_mixer(xs, bias_tiles, l0_attn_norm, l0_w_in, l0_q_norm, l0_k_norm, l0_w_out)
        xs = conv_ffn(xs, l0_ffn_norm, l0_w_up, l0_conv_w, l0_conv_b, l0_w_down)
        xs = fox_mixer(xs, l1_attn_norm, l1_w_in, l1_b_f, l1_q_norm, l1_k_norm, l1_w_out)
        xs = conv_ffn(xs, l1_ffn_norm, l1_w_up, l1_conv_w, l1_conv_b, l1_w_down)
        xs = mla_mixer(xs, positions[bi], l2_attn_norm, l2_w_in, l2_q_a_norm, l2_kv_a_norm,
                       l2_w_q_b, l2_w_kv_b, l2_q_norm, l2_k_norm, l2_w_out)
        xs = conv_ffn(xs, l2_ffn_norm, l2_w_up, l2_conv_w, l2_conv_b, l2_w_down)
        xs = dilated_mixer(xs, bias_tiles, l3_attn_norm, l3_w_in, l3_q_norm, l3_k_norm, l3_w_out)
        xs = conv_ffn(xs, l3_ffn_norm, l3_w_up, l3_conv_w, l3_conv_b, l3_w_down)
        outs.append(xs)
    return jnp.stack(outs)
```

```python
import functools
import math

import jax
import jax.numpy as jnp
from jax import lax
from jax.experimental import pallas as pl
from jax.experimental.pallas import tpu as pltpu

F32 = jnp.float32
BF16 = jnp.bfloat16

V7X_LANES = 128
V7X_MXU_DIM = 256
V7X_VMEM_BYTES = 64 * 1024 * 1024
VMEM_LIMIT_BYTES = V7X_VMEM_BYTES - 8 * 1024 * 1024

HEAD_DIM = 128
RMS_EPS = 1e-6
DIL_CONFIGS = ((128, 1), (512, 4), (2048, 16))
DIL_HEADS = 16
DIL_SPAN = 128
BLOCK_Q_DIL = 128
NUM_BUCKETS = 32
MAX_DISTANCE = 2048
FOX_HEADS = 32
MLA_HEADS = 64
MLA_Q_RANK = 1536
MLA_KV_RANK = 512
MLA_NOPE_DIM = 128
MLA_ROPE_DIM = 64
MLA_V_DIM = 128
MLA_QK_DIM = MLA_NOPE_DIM + MLA_ROPE_DIM
QK_PAD = 2 * V7X_LANES
ROPE_THETA = 10000.0
D_FF = 11008
MASK_VALUE = -1e30
LOG2E = math.log2(math.e)
FLASH_TK = 512
N_SPLIT = 3


def _cparams(n_axes):
    return pltpu.CompilerParams(dimension_semantics=("arbitrary",) * n_axes,
                                vmem_limit_bytes=VMEM_LIMIT_BYTES)


def _tile(dim, pref):
    t = min(dim, pref)
    while dim % t:
        t //= 2
    return t


def _split_bf16(x):
    terms = []
    for _ in range(N_SPLIT - 1):
        t = x.astype(BF16)
        terms.append(t)
        x = x - t.astype(F32)
    terms.append(x.astype(BF16))
    return terms


def _rms_scale(x):
    return lax.rsqrt(jnp.mean(x * x, axis=-1, keepdims=True) + RMS_EPS)


def _rmsnorm_kernel(x_ref, g_ref, o_ref):
    x = x_ref[...].astype(F32)
    o_ref[...] = (x * _rms_scale(x) * g_ref[...]).astype(o_ref.dtype)


def rmsnorm(x, gain, *, width=None, col_block=0, rows=256):
    s, wtot = x.shape
    width = wtot if width is None else width
    tr = _tile(s, rows)
    return pl.pallas_call(
        _rmsnorm_kernel,
        grid=(s // tr,),
        in_specs=[pl.BlockSpec((tr, width), lambda i: (i, col_block)),
                  pl.BlockSpec((1, width), lambda i: (0, 0))],
        out_specs=pl.BlockSpec((tr, width), lambda i: (i, 0)),
        out_shape=jax.ShapeDtypeStruct((s, width), BF16),
        compiler_params=_cparams(1),
        name="rmsnorm",
    )(x, gain.reshape(1, width).astype(F32))


def _residue_perm(tr, d, to_residue_major):
    n = tr // d
    out_row = lax.broadcasted_iota(jnp.int32, (tr, tr), 0)
    in_row = lax.broadcasted_iota(jnp.int32, (tr, tr), 1)
    if to_residue_major:
        src = (out_row & (n - 1)) * d + (out_row >> (n.bit_length() - 1))
    else:
        src = (out_row & (d - 1)) * n + (out_row >> (d.bit_length() - 1))
    return jnp.where(in_row == src, 1.0, 0.0).astype(BF16)


def _rmsnorm_residue_kernel(x_ref, g_ref, *o_refs, dilations):
    tr = x_ref.shape[0]
    x = x_ref[...]
    hn = (x * _rms_scale(x) * g_ref[...]).astype(BF16)
    for o_ref, d in zip(o_refs, dilations):
        if d == 1:
            o_ref[0] = hn
            continue
        n = tr // d
        moved = jnp.dot(_residue_perm(tr, d, True), hn, preferred_element_type=F32).astype(BF16)
        for r in range(d):
            o_ref[r] = moved[r * n:(r + 1) * n, :]


def rmsnorm_residue_major(x, gain, dilations, *, rows=256):
    s, w = x.shape
    tr = _tile(s, rows)
    outs = pl.pallas_call(
        functools.partial(_rmsnorm_residue_kernel, dilations=dilations),
        grid=(s // tr,),
        in_specs=[pl.BlockSpec((tr, w), lambda i: (i, 0)),
                  pl.BlockSpec((1, w), lambda i: (0, 0))],
        out_specs=[pl.BlockSpec((d, tr // d, w), lambda i: (0, i, 0)) for d in dilations],
        out_shape=[jax.ShapeDtypeStruct((d, s // d, w), BF16) for d in dilations],
        compiler_params=_cparams(1),
        name="rmsnorm_residue",
    )(x, gain.reshape(1, w).astype(F32))
    return [o.reshape(s, w) for o in outs]


def _mm_kernel(*refs, nk, has_res):
    if has_res:
        a_ref, w_ref, r_ref, o_ref = refs[:4]
        scratch = refs[4:]
    else:
        a_ref, w_ref, o_ref = refs[:3]
        r_ref = None
        scratch = refs[3:]

    def finish(acc):
        if has_res:
            acc = acc + r_ref[...]
        o_ref[...] = acc.astype(o_ref.dtype)

    if nk == 1:
        finish(jnp.dot(a_ref[...], w_ref[...], preferred_element_type=F32))
        return

    acc_ref = scratch[0]
    k = pl.program_id(2)

    @pl.when(k == 0)
    def _():
        acc_ref[...] = jnp.zeros_like(acc_ref)

    acc_ref[...] += jnp.dot(a_ref[...], w_ref[...], preferred_element_type=F32)

    @pl.when(k == nk - 1)
    def _():
        finish(acc_ref[...])


def matmul(a, w, *, n=None, w_col0=0, residual=None, out_dtype=BF16, tm=1024, tn=1024, tk=4096):
    m, kdim = a.shape
    n = w.shape[1] if n is None else n
    tm, tn, tk = _tile(m, tm), _tile(n, tn), _tile(kdim, tk)
    nk = kdim // tk
    j0 = w_col0 // tn
    in_specs = [pl.BlockSpec((tm, tk), lambda i, j, k: (i, k)),
                pl.BlockSpec((tk, tn), lambda i, j, k: (k, j + j0))]
    args = [a, w]
    if residual is not None:
        in_specs.append(pl.BlockSpec((tm, tn), lambda i, j, k: (i, j)))
        args.append(residual)
    return pl.pallas_call(
        functools.partial(_mm_kernel, nk=nk, has_res=residual is not None),
        grid=(m // tm, n // tn, nk),
        in_specs=in_specs,
        out_specs=pl.BlockSpec((tm, tn), lambda i, j, k: (i, j)),
        out_shape=jax.ShapeDtypeStruct((m, n), out_dtype),
        scratch_shapes=[pltpu.VMEM((tm, tn), F32)] if nk > 1 else [],
        compiler_params=_cparams(3),
        name="matmul",
    )(*args)


def _t5_causal_bucket(dist):
    max_exact = NUM_BUCKETS // 2
    d32 = jnp.maximum(dist, 1).astype(F32)
    large = max_exact + (jnp.log(d32 / max_exact) / math.log(MAX_DISTANCE / max_exact)
                         * (NUM_BUCKETS - max_exact)).astype(jnp.int32)
    large = jnp.minimum(large, NUM_BUCKETS - 1)
    return jnp.where(dist < max_exact, dist, large)


def _dil_bucket_onehot():
    band = BLOCK_Q_DIL + DIL_SPAN
    rel = jnp.arange(BLOCK_Q_DIL)[:, None] + DIL_SPAN - jnp.arange(band)[None, :]
    valid = ((rel >= 0) & (rel <= DIL_SPAN)).astype(F32)
    relc = jnp.clip(rel, 0, DIL_SPAN)
    hots = []
    for window, dilation in DIL_CONFIGS:
        dist = jnp.arange(window // dilation + 1, dtype=jnp.int32) * dilation
        bucket = _t5_causal_bucket(dist)[relc]
        hots.append((bucket[None] == jnp.arange(NUM_BUCKETS)[:, None, None]).astype(F32))
    return jnp.stack(hots), valid


def _dil_bias_kernel(rb_ref, oh_ref, valid_ref, o_ref):
    g = pl.program_id(0)
    valid = valid_ref[...] > 0
    for h in range(DIL_HEADS):
        def body(b, acc, h=h):
            return acc + rb_ref[b, g * DIL_HEADS + h] * oh_ref[0, b]
        acc = lax.fori_loop(0, NUM_BUCKETS, body, jnp.zeros(valid_ref.shape, F32))
        o_ref[0, h] = jnp.where(valid, acc, -jnp.inf)


def dil_bias_tiles(rel_bias):
    onehot, valid = _dil_bucket_onehot()
    ng = len(DIL_CONFIGS)
    band = BLOCK_Q_DIL + DIL_SPAN
    return pl.pallas_call(
        _dil_bias_kernel,
        grid=(ng,),
        in_specs=[pl.BlockSpec(memory_space=pltpu.SMEM),
                  pl.BlockSpec((1, NUM_BUCKETS, BLOCK_Q_DIL, band), lambda g: (g, 0, 0, 0)),
                  pl.BlockSpec((BLOCK_Q_DIL, band), lambda g: (0, 0))],
        out_specs=pl.BlockSpec((1, DIL_HEADS, BLOCK_Q_DIL, band), lambda g: (g, 0, 0, 0)),
        out_shape=jax.ShapeDtypeStruct((ng, DIL_HEADS, BLOCK_Q_DIL, band), F32),
        compiler_params=_cparams(1),
        name="dil_bias",
    )(rel_bias.astype(F32), onehot, valid)


def _dil_attn_kernel(q_ref, kp_ref, kc_ref, vp_ref, vc_ref, bias_ref, qg_ref, kg_ref, o_ref, l_ref):
    has_prev = pl.program_id(1) > 0
    tq = q_ref.shape[0]
    scale = HEAD_DIM ** -0.5
    qg = qg_ref[...] * scale
    kg = kg_ref[...]
    col = lax.broadcasted_iota(jnp.int32, (tq, 2 * tq), 1)
    keep = (col >= tq) | has_prev
    for h in range(DIL_HEADS):
        sl = slice(h * HEAD_DIM, (h + 1) * HEAD_DIM)
        q = q_ref[:, sl].astype(F32)
        qn = q * _rms_scale(q) * qg
        k = jnp.concatenate([kp_ref[:, sl], kc_ref[:, sl]], axis=0).astype(F32)
        kn = k * _rms_scale(k) * kg
        s = lax.dot_general(qn.astype(BF16), kn.astype(BF16), (((1,), (1,)), ((), ())),
                            preferred_element_type=F32)
        s = jnp.where(keep, s + bias_ref[h], -jnp.inf)
        m = jnp.max(s, axis=-1, keepdims=True)
        p = jnp.exp(s - m)
        l = jnp.sum(p, axis=-1, keepdims=True)
        v = jnp.concatenate([vp_ref[:, sl], vc_ref[:, sl]], axis=0)
        o = jnp.dot(p.astype(BF16), v, preferred_element_type=F32) / l
        o_ref[:, sl] = o.astype(o_ref.dtype)
        l_ref[:, sl] = jnp.broadcast_to(m + jnp.log(l), (tq, HEAD_DIM))


def dil_attention_group(qkv, bias_g, q_gain, k_gain, dilation):
    s = qkv.shape[0]
    hw = DIL_HEADS * HEAD_DIM
    tq = BLOCK_Q_DIL
    nb = s // dilation // tq

    def cur(c):
        return lambda r, i: (r * nb + i, c)

    def prev(c):
        return lambda r, i: (r * nb + jnp.maximum(i - 1, 0), c)

    blk = (tq, hw)
    return pl.pallas_call(
        _dil_attn_kernel,
        grid=(dilation, nb),
        in_specs=[pl.BlockSpec(blk, cur(0)),
                  pl.BlockSpec(blk, prev(1)), pl.BlockSpec(blk, cur(1)),
                  pl.BlockSpec(blk, prev(2)), pl.BlockSpec(blk, cur(2)),
                  pl.BlockSpec((DIL_HEADS, tq, 2 * tq), lambda r, i: (0, 0, 0)),
                  pl.BlockSpec((1, HEAD_DIM), lambda r, i: (0, 0)),
                  pl.BlockSpec((1, HEAD_DIM), lambda r, i: (0, 0))],
        out_specs=[pl.BlockSpec(blk, cur(0)), pl.BlockSpec(blk, cur(0))],
        out_shape=[jax.ShapeDtypeStruct((s, hw), F32), jax.ShapeDtypeStruct((s, hw), F32)],
        compiler_params=_cparams(2),
        name="dil_attn",
    )(qkv, qkv, qkv, qkv, qkv, bias_g,
      q_gain.reshape(1, HEAD_DIM).astype(F32), k_gain.reshape(1, HEAD_DIM).astype(F32))


def _dil_merge_kernel(*refs, dilations):
    ng = len(dilations)
    o_refs, l_refs, out_ref = refs[:ng], refs[ng:2 * ng], refs[2 * ng]
    tr, hw = out_ref.shape

    def token_order(ref, d):
        if d == 1:
            return ref[0]
        perm = _residue_perm(tr, d, False)
        x = jnp.concatenate([ref[r] for r in range(d)], axis=0)
        acc = jnp.zeros((tr, hw), F32)
        for term in _split_bf16(x):
            acc = acc + jnp.dot(perm, term, preferred_element_type=F32)
        return acc

    outs = [token_order(ref, d) for ref, d in zip(o_refs, dilations)]
    lses = [token_order(ref, d) for ref, d in zip(l_refs, dilations)]
    m = functools.reduce(jnp.maximum, lses)
    es = [jnp.exp(l - m) for l in lses]
    den = functools.reduce(jnp.add, es)
    out = functools.reduce(jnp.add, [(e / den) * o for e, o in zip(es, outs)])
    out_ref[...] = out.astype(out_ref.dtype)


def dil_merge(outs, lses, dilations):
    s, hw = outs[0].shape
    tr = _tile(s, 256)
    views = [a.reshape(d, s // d, hw) for a, d in zip(list(outs) + list(lses), list(dilations) * 2)]
    specs = [pl.BlockSpec((d, tr // d, hw), lambda i: (0, i, 0)) for d in list(dilations) * 2]
    return pl.pallas_call(
        functools.partial(_dil_merge_kernel, dilations=tuple(dilations)),
        grid=(s // tr,),
        in_specs=specs,
        out_specs=pl.BlockSpec((tr, hw), lambda i: (i, 0)),
        out_shape=jax.ShapeDtypeStruct((s, hw), BF16),
        compiler_params=_cparams(1),
        name="dil_merge",
    )(*views)


def dilated_mixer(x, bias_tiles, norm_g, w_in, q_norm, k_norm, w_out):
    dilations = tuple(d for _, d in DIL_CONFIGS)
    gw = 3 * DIL_HEADS * HEAD_DIM
    hs = rmsnorm_residue_major(x, norm_g, dilations)
    w_in = w_in.astype(BF16)
    outs, lses = [], []
    for g, (h, dilation) in enumerate(zip(hs, dilations)):
        qkv = matmul(h, w_in, n=gw, w_col0=g * gw)
        o, lse = dil_attention_group(qkv, bias_tiles[g], q_norm, k_norm, dilation)
        outs.append(o)
        lses.append(lse)
    merged = dil_merge(outs, lses, dilations)
    return matmul(merged, w_out.astype(BF16), residual=x, out_dtype=F32)


def _rotate_half_tile(y, c_ref, s1_ref, s2_ref):
    return y * c_ref[...] + pltpu.roll(y, 96, 1) * s1_ref[...] + pltpu.roll(y, 32, 1) * s2_ref[...]


def _head_prep_kernel(*refs, npieces, dn, scale, rope, ones_tile):
    xs = [refs[p][...].astype(F32) for p in range(npieces)]
    g_ref = refs[npieces]
    nxt = npieces + 1
    if rope:
        tabs = refs[nxt:nxt + 3]
        nxt += 3
    o_ref = refs[nxt]
    ssq = jnp.sum(xs[0] * xs[0], axis=-1, keepdims=True)
    for x in xs[1:]:
        ssq = ssq + jnp.sum(x * x, axis=-1, keepdims=True)
    r = lax.rsqrt(ssq * (1.0 / dn) + RMS_EPS) * scale
    for p, x in enumerate(xs):
        sl = slice(p * V7X_LANES, (p + 1) * V7X_LANES)
        y = x * r * g_ref[:, sl]
        if rope and p == npieces - 1:
            y = _rotate_half_tile(y, *tabs)
        o_ref[:, sl] = y.astype(o_ref.dtype)
    if ones_tile:
        lane = lax.broadcasted_iota(jnp.int32, (o_ref.shape[0], V7X_LANES), 1)
        o_ref[:, npieces * V7X_LANES:] = jnp.where(lane < N_SPLIT, 1.0, 0.0).astype(o_ref.dtype)


def head_prep(pieces, gain, n_heads, *, dn, scale=1.0, rope_tables=None, ones_tile=False, rows=1024):
    s = pieces[0][0].shape[0]
    tr = _tile(s, rows)
    npieces = len(pieces)
    w_in = npieces * V7X_LANES
    w_out = w_in + (V7X_LANES if ones_tile else 0)
    in_specs = [pl.BlockSpec((tr, V7X_LANES), (lambda i, h, f=f: (i, f(h)))) for _, f in pieces]
    args = [a for a, _ in pieces]
    in_specs.append(pl.BlockSpec((1, w_in), lambda i, h: (0, 0)))
    args.append(gain.reshape(1, w_in).astype(F32))
    if rope_tables is not None:
        for t in rope_tables:
            in_specs.append(pl.BlockSpec((tr, V7X_LANES), lambda i, h: (i, 0)))
            args.append(t)
    return pl.pallas_call(
        functools.partial(_head_prep_kernel, npieces=npieces, dn=dn, scale=scale,
                          rope=rope_tables is not None, ones_tile=ones_tile),
        grid=(s // tr, n_heads),
        in_specs=in_specs,
        out_specs=pl.BlockSpec((tr, w_out), lambda i, h: (i, h)),
        out_shape=jax.ShapeDtypeStruct((s, n_heads * w_out), BF16),
        compiler_params=_cparams(2),
        name="head_prep",
    )(*args)


ONES_ROWS = 16


def _flash_kernel(*refs, tq, tk, dv, has_g, n_chains, depth):
    if has_g:
        g_ref, q_ref, k_ref, v_ref, o_ref, acc_ref, s_ref = refs
    else:
        q_ref, k_ref, v_ref, o_ref, acc_ref, s_ref = refs
        g_ref = None
    h = pl.program_id(0)
    i = pl.program_id(1)
    ratio = tq // tk
    cw = tq // n_chains
    acc_ref[...] = jnp.zeros(acc_ref.shape, F32)
    ones = jnp.ones((ONES_ROWS, tk), BF16)

    def live_chains(diag):
        return [c for c in range(n_chains) if diag is None or (c + 1) * cw > diag * tk]

    def k_block(j):
        return k_ref[pl.ds(pl.multiple_of(j * tk, tk), tk), :]

    def qk(kb, c):
        return lax.dot_general(kb, q_ref[c * cw:(c + 1) * cw, :], (((1,), (1,)), ((), ())),
                               preferred_element_type=F32)

    def step(j, m_old, diag, next_diag):
        kb = k_block(j)
        vtb = jnp.concatenate([v_ref[pl.ds(pl.multiple_of(j * tk, tk), tk), :].T, ones], axis=0)
        if has_g:
            g = g_ref[h, i * ratio] - g_ref[h, j]
        rd = (j % 2) * depth
        wr = depth - rd
        live = live_chains(diag)
        nxt = []
        if next_diag is not None:
            nxt = live_chains(None if next_diag == "full" else next_diag)[:depth]
            kb_next = k_block(j + 1)

        def logits(n):
            c = live[n]
            s_t = s_ref[rd + n] if n < depth else qk(kb, c)
            if diag is not None and c * cw < (diag + 1) * tk:
                key = lax.broadcasted_iota(jnp.int32, (tk, cw), 0) + diag * tk
                qry = lax.broadcasted_iota(jnp.int32, (tk, cw), 1) + c * cw
                s_t = jnp.where(key <= qry, s_t, MASK_VALUE)
            return s_t

        jobs = [("own", n) for n in range(depth, len(live))] + [("next", n) for n in range(len(nxt))]
        pending = {}

        def issue():
            if jobs:
                kind, n = jobs.pop(0)
                if kind == "own":
                    pending[n] = logits(n)
                else:
                    s_ref[wr + n] = qk(kb_next, nxt[n])

        for _ in range(depth):
            issue()
        m_parts = {}
        for n, c in enumerate(live):
            cs = slice(c * cw, (c + 1) * cw)
            s_t = logits(n) if n < depth else pending.pop(n)
            mb = jnp.max(s_t, axis=0, keepdims=True)
            if has_g:
                mb = mb + g
            mo = m_old[:, cs]
            mn = jnp.maximum(mo, mb)
            alpha = jnp.exp2(mo - mn)
            sub = (mn - g) if has_g else mn
            p_t = jnp.exp2(s_t - sub).astype(BF16)
            issue()
            acc_ref[:, cs] = alpha * acc_ref[:, cs] + jnp.dot(vtb, p_t, preferred_element_type=F32)
            m_parts[c] = mn
        while jobs:
            issue()
        parts = [m_parts[c] if c in m_parts else m_old[:, c * cw:(c + 1) * cw] for c in range(n_chains)]
        return jnp.concatenate(parts, axis=1) if n_chains > 1 else parts[0]

    kb0 = k_block(0)
    for slot in range(depth):
        s_ref[slot] = qk(kb0, slot)
    m = jnp.full((1, tq), MASK_VALUE, F32)
    m = lax.fori_loop(0, i * ratio, lambda j, mm: step(j, mm, None, "full"), m)
    for d in range(ratio):
        m = step(i * ratio + d, m, d, d + 1 if d + 1 < ratio else None)
    acc = acc_ref[...]
    o_t = acc[:dv, :] / acc[dv:dv + 1, :]
    o_ref[...] = o_t.T.astype(o_ref.dtype)


def flash_attention(q, k, v, n_heads, *, dv, v_col, block_offsets=None, tq=2048, depth=2):
    s = q.shape[0]
    tk = _tile(s, FLASH_TK)
    tq = max(_tile(s, tq), tk)
    n_chains = max(tq // V7X_MXU_DIM, depth)
    in_specs = [pl.BlockSpec((tq, QK_PAD), lambda h, i: (i, h)),
                pl.BlockSpec((s, QK_PAD), lambda h, i: (0, h)),
                pl.BlockSpec((s, dv), lambda h, i: (0, v_col(h)))]
    args = [q, k, v]
    if block_offsets is not None:
        in_specs.insert(0, pl.BlockSpec(memory_space=pltpu.SMEM))
        args.insert(0, block_offsets)
    return pl.pallas_call(
        functools.partial(_flash_kernel, tq=tq, tk=tk, dv=dv, has_g=block_offsets is not None,
                          n_chains=n_chains, depth=depth),
        grid=(n_heads, s // tq),
        in_specs=in_specs,
        out_specs=pl.BlockSpec((tq, dv), lambda h, i: (i, h)),
        out_shape=jax.ShapeDtypeStruct((s, n_heads * dv), BF16),
        scratch_shapes=[pltpu.VMEM((dv + ONES_ROWS, tq), F32),
                        pltpu.VMEM((2 * depth, tk, tq // n_chains), F32)],
        compiler_params=_cparams(2),
        name="flash_attn",
    )(*args)


def _fox_gate_kernel(h_ref, wg_ref, bf_ref, o_ref, carry_ref):
    i = pl.program_id(0)
    ts = h_ref.shape[0]

    @pl.when(i == 0)
    def _():
        carry_ref[...] = jnp.zeros_like(carry_ref)

    z = jnp.dot(h_ref[...], wg_ref[...], preferred_element_type=F32) + bf_ref[...]
    lf = jnp.minimum(z, 0.0) - jnp.log(1.0 + jnp.exp(-jnp.abs(z)))
    row = lax.broadcasted_iota(jnp.int32, (ts, ts), 0)
    colm = lax.broadcasted_iota(jnp.int32, (ts, ts), 1)
    lower = jnp.where(row >= colm, 1.0, 0.0).astype(BF16)
    cs = carry_ref[0:1, :]
    for term in _split_bf16(lf):
        cs = cs + jnp.dot(lower, term, preferred_element_type=F32)
    carry_ref[...] = jnp.broadcast_to(cs[ts - 1:ts, :], carry_ref.shape)
    o_ref[...] = cs * LOG2E


def fox_cum_log_gate(h, w_gate, b_f):
    s, d = h.shape
    ts = _tile(s, 512)
    return pl.pallas_call(
        _fox_gate_kernel,
        grid=(s // ts,),
        in_specs=[pl.BlockSpec((ts, d), lambda i: (i, 0)),
                  pl.BlockSpec((d, V7X_LANES), lambda i: (0, 0)),
                  pl.BlockSpec((1, V7X_LANES), lambda i: (0, 0))],
        out_specs=pl.BlockSpec((ts, V7X_LANES), lambda i: (i, 0)),
        out_shape=jax.ShapeDtypeStruct((s, V7X_LANES), F32),
        scratch_shapes=[pltpu.VMEM((8, V7X_LANES), F32)],
        compiler_params=_cparams(1),
        name="fox_gate",
    )(h, w_gate, b_f)


def _fox_k_prep_kernel(k_ref, cl_ref, g_ref, o_ref):
    h = pl.program_id(1)
    x = k_ref[...].astype(F32)
    o_ref[:, :V7X_LANES] = (x * _rms_scale(x) * g_ref[...]).astype(o_ref.dtype)
    lane = lax.broadcasted_iota(jnp.int32, cl_ref.shape, 1)
    col = jnp.sum(jnp.where(lane == h, cl_ref[...], 0.0), axis=-1, keepdims=True)
    d = col[0:1, :] - col
    y = jnp.zeros(cl_ref.shape, F32)
    for n, term in enumerate(_split_bf16(d)):
        y = jnp.where(lane == n, term.astype(F32), y)
    o_ref[:, V7X_LANES:] = y.astype(o_ref.dtype)


def fox_k_prep(proj, cl, gain, col0):
    s = proj.shape[0]
    tr = _tile(s, FLASH_TK)
    return pl.pallas_call(
        _fox_k_prep_kernel,
        grid=(s // tr, FOX_HEADS),
        in_specs=[pl.BlockSpec((tr, V7X_LANES), lambda i, h: (i, col0 + h)),
                  pl.BlockSpec((tr, V7X_LANES), lambda i, h: (i, 0)),
                  pl.BlockSpec((1, V7X_LANES), lambda i, h: (0, 0))],
        out_specs=pl.BlockSpec((tr, QK_PAD), lambda i, h: (i, h)),
        out_shape=jax.ShapeDtypeStruct((s, FOX_HEADS * QK_PAD), BF16),
        compiler_params=_cparams(2),
        name="fox_k_prep",
    )(proj, cl, gain.reshape(1, V7X_LANES).astype(F32))


def fox_mixer(x, norm_g, w_in, b_f, q_norm, k_norm, w_out):
    s = x.shape[0]
    width = FOX_HEADS * HEAD_DIM
    tk = _tile(s, FLASH_TK)
    h = rmsnorm(x, norm_g)
    w_in = w_in.astype(BF16)
    proj = matmul(h, w_in, n=3 * width)
    w_gate = jnp.pad(w_in[:, 3 * width:], ((0, 0), (0, V7X_LANES - FOX_HEADS)))
    b_gate = jnp.pad(b_f.astype(F32), (0, V7X_LANES - FOX_HEADS)).reshape(1, V7X_LANES)
    cl = fox_cum_log_gate(h, w_gate, b_gate)
    q = head_prep([(proj, lambda hh: hh)], q_norm, FOX_HEADS, dn=HEAD_DIM,
                  scale=HEAD_DIM ** -0.5 * LOG2E, ones_tile=True)
    k = fox_k_prep(proj, cl, k_norm, FOX_HEADS)
    offsets = cl[::tk, :FOX_HEADS].T
    o = flash_attention(q, k, proj, FOX_HEADS, dv=HEAD_DIM, v_col=lambda hh: 2 * FOX_HEADS + hh,
                        block_offsets=offsets)
    return matmul(o, w_out.astype(BF16), residual=x, out_dtype=F32, tk=2048)


def _rope_table_kernel(pos_ref, freq_ref, c_ref, s1_ref, s2_ref):
    ang = pos_ref[...].astype(F32) * freq_ref[...]
    lane = lax.broadcasted_iota(jnp.int32, ang.shape, 1)
    half = MLA_ROPE_DIM // 2
    c = jnp.cos(ang)
    sn = jnp.sin(ang)
    c_ref[...] = jnp.where(lane < 2 * half, c, 0.0)
    s1_ref[...] = jnp.where(lane < half, -sn, 0.0)
    s2_ref[...] = jnp.where((lane >= half) & (lane < 2 * half), sn, 0.0)


def rope_tables(positions):
    s = positions.shape[0]
    half = MLA_ROPE_DIM // 2
    freqs = ROPE_THETA ** (-jnp.arange(half, dtype=F32) / half)
    freq_row = jnp.concatenate([freqs, freqs, jnp.zeros((V7X_LANES - 2 * half,), F32)]).reshape(1, V7X_LANES)
    tr = _tile(s, 1024)
    spec = pl.BlockSpec((tr, V7X_LANES), lambda i: (i, 0))
    return pl.pallas_call(
        _rope_table_kernel,
        grid=(s // tr,),
        in_specs=[pl.BlockSpec((tr, 1), lambda i: (i, 0)), pl.BlockSpec((1, V7X_LANES), lambda i: (0, 0))],
        out_specs=[spec] * 3,
        out_shape=[jax.ShapeDtypeStruct((s, V7X_LANES), F32)] * 3,
        compiler_params=_cparams(1),
        name="rope_tables",
    )(positions.reshape(s, 1), freq_row)


def _pad_cols(w, to):
    return jnp.pad(w, ((0, 0), (0, to - w.shape[1])))


def mla_mixer(x, positions, norm_g, w_in, q_a_norm, kv_a_norm, w_q_b, w_kv_b, q_norm, k_norm, w_out):
    lat_w = MLA_Q_RANK + MLA_KV_RANK
    in_w = 2304
    h = rmsnorm(x, norm_g)
    proj = matmul(h, _pad_cols(w_in, in_w).astype(BF16), out_dtype=F32, tn=1152)
    q_lat = rmsnorm(proj, q_a_norm, width=MLA_Q_RANK, col_block=0)
    kv_lat = rmsnorm(proj, kv_a_norm, width=MLA_KV_RANK, col_block=MLA_Q_RANK // MLA_KV_RANK)
    wq = jnp.pad(w_q_b.reshape(MLA_Q_RANK, MLA_HEADS, MLA_QK_DIM), ((0, 0), (0, 0), (0, QK_PAD - MLA_QK_DIM)))
    q_raw = matmul(q_lat, wq.reshape(MLA_Q_RANK, MLA_HEADS * QK_PAD).astype(BF16))
    kv_raw = matmul(kv_lat, w_kv_b.astype(BF16))
    tabs = rope_tables(positions)
    qg = jnp.pad(q_norm, (0, QK_PAD - MLA_QK_DIM))
    kg = jnp.pad(k_norm, (0, QK_PAD - MLA_QK_DIM))
    q = head_prep([(q_raw, lambda hh: 2 * hh), (q_raw, lambda hh: 2 * hh + 1)], qg, MLA_HEADS,
                  dn=MLA_QK_DIM, scale=MLA_QK_DIM ** -0.5 * LOG2E, rope_tables=tabs)
    k = head_prep([(kv_raw, lambda hh: 2 * hh), (proj, lambda hh: lat_w // V7X_LANES)], kg, MLA_HEADS,
                  dn=MLA_QK_DIM, rope_tables=tabs)
    o = flash_attention(q, k, kv_raw, MLA_HEADS, dv=MLA_V_DIM, v_col=lambda hh: 2 * hh + 1)
    return matmul(o, w_out.astype(BF16), residual=x, out_dtype=F32, tk=2048)


def _ffn_up_kernel(a_ref, wg_ref, wv_ref, cwg_ref, cwv_ref, cbg_ref, cbv_ref, o_ref, carry_ref):
    i = pl.program_id(0)
    j = pl.program_id(1)
    tm = a_ref.shape[0]

    @pl.when(i == 0)
    def _():
        carry_ref[j] = jnp.zeros(carry_ref.shape[1:], F32)

    a = a_ref[...]
    row = lax.broadcasted_iota(jnp.int32, (tm, o_ref.shape[1]), 0)

    def conv(u, cw_ref, cb_ref, slot):
        prev = carry_ref[j, slot]
        carry_ref[j, slot] = u[tm - 8:, :]
        u1 = jnp.where(row == 0, prev[7:8, :], pltpu.roll(u, 1, 0))
        u2 = jnp.where(row == 0, prev[6:7, :], jnp.where(row == 1, prev[7:8, :], pltpu.roll(u, 2, 0)))
        return cw_ref[2:3, :] * u + cw_ref[1:2, :] * u1 + cw_ref[0:1, :] * u2 + cb_ref[...]

    gate = conv(jnp.dot(a, wg_ref[...], preferred_element_type=F32), cwg_ref, cbg_ref, 0)
    val = conv(jnp.dot(a, wv_ref[...], preferred_element_type=F32), cwv_ref, cbv_ref, 1)
    o_ref[...] = (gate * (1.0 / (1.0 + jnp.exp(-gate))) * val).astype(o_ref.dtype)


def ffn_up(h, w_up, conv_w, conv_b, *, tm=1024, tn=256):
    s, d = h.shape
    f = w_up.shape[1] // 2
    tm = _tile(s, tm)
    nj = f // tn
    cw = conv_w.astype(F32)
    cb = conv_b.reshape(1, 2 * f).astype(F32)
    return pl.pallas_call(
        _ffn_up_kernel,
        grid=(s // tm, nj),
        in_specs=[pl.BlockSpec((tm, d), lambda i, j: (i, 0)),
                  pl.BlockSpec((d, tn), lambda i, j: (0, j)),
                  pl.BlockSpec((d, tn), lambda i, j: (0, j + nj)),
                  pl.BlockSpec((3, tn), lambda i, j: (0, j)),
                  pl.BlockSpec((3, tn), lambda i, j: (0, j + nj)),
                  pl.BlockSpec((1, tn), lambda i, j: (0, j)),
                  pl.BlockSpec((1, tn), lambda i, j: (0, j + nj))],
        out_specs=pl.BlockSpec((tm, tn), lambda i, j: (i, j)),
        out_shape=jax.ShapeDtypeStruct((s, f), BF16),
        scratch_shapes=[pltpu.VMEM((nj, 2, 8, tn), F32)],
        compiler_params=_cparams(2),
        name="ffn_up",
    )(h, w_up, w_up, cw, cw, cb, cb)


def conv_ffn(x, norm_g, w_up, conv_w, conv_b, w_down):
    h = rmsnorm(x, norm_g)
    y = ffn_up(h, w_up.astype(BF16), conv_w, conv_b)
    return matmul(y, w_down.astype(BF16), residual=x, out_dtype=F32, tn=512, tk=w_down.shape[0] // 2)


def kernel(x, positions, rel_bias,
           l0_attn_norm, l0_w_in, l0_q_norm, l0_k_norm, l0_w_out,
           l0_ffn_norm, l0_w_up, l0_conv_w, l0_conv_b, l0_w_down,
           l1_attn_norm, l1_w_in, l1_b_f, l1_q_norm, l1_k_norm, l1_w_out,
           l1_ffn_norm, l1_w_up, l1_conv_w, l1_conv_b, l1_w_down,
           l2_attn_norm, l2_w_in, l2_q_a_norm, l2_kv_a_norm, l2_w_q_b, l2_w_kv_b, l2_q_norm, l2_k_norm, l2_w_out,
           l2_ffn_norm, l2_w_up, l2_conv_w, l2_conv_b, l2_w_down,
           l3_attn_norm, l3_w_in, l3_q_norm, l3_k_norm, l3_w_out,
           l3_ffn_norm, l3_w_up, l3_conv_w, l3_conv_b, l3_w_down):
    b, s, d = x.shape
    outs = []
    bias_tiles = dil_bias_tiles(rel_bias)
    for bi in range(b):
        xs = x[bi]
        xs = dilated_mixer(xs, bias_tiles, l0_attn_norm, l0_w_in, l0_q_norm, l0_k_norm, l0_w_out)
        xs = conv_ffn(xs, l0_ffn_norm, l0_w_up, l0_conv_w, l0_conv_b, l0_w_down)
        xs = fox_mixer(xs, l1_attn_norm, l1_w_in, l1_b_f, l1_q_norm, l1_k_norm, l1_w_out)
        xs = conv_ffn(xs, l1_ffn_norm, l1_w_up, l1_conv_w, l1_conv_b, l1_w_down)
        xs = mla_mixer(xs, positions[bi], l2_attn_norm, l2_w_in, l2_q_a_norm, l2_kv_a_norm,
                       l2_w_q_b, l2_w_kv_b, l2_q_norm, l2_k_norm, l2_w_out)
        xs = conv_ffn(xs, l2_ffn_norm, l2_w_up, l2_conv_w, l2_conv_b, l2_w_down)
        xs = dilated_mixer(xs, bias_tiles, l3_attn_norm, l3_w_in, l3_q_norm, l3_k_norm, l3_w_out)
        xs = conv_ffn(xs, l3_ffn_norm, l3_w_up, l3_conv_w, l3_conv_b, l3_w_down)
        outs.append(xs)
    return jnp.stack(outs)
```

```python
import functools
import math

import jax
import jax.numpy as jnp
from jax import lax
from jax.experimental import pallas as pl
from jax.experimental.pallas import tpu as pltpu

F32 = jnp.float32
BF16 = jnp.bfloat16

V7X_LANES = 128
V7X_MXU_DIM = 256
V7X_VMEM_BYTES = 64 * 1024 * 1024
VMEM_LIMIT_BYTES = V7X_VMEM_BYTES - 8 * 1024 * 1024

HEAD_DIM = 128
RMS_EPS = 1e-6
DIL_CONFIGS = ((128, 1), (512, 4), (2048, 16))
DIL_HEADS = 16
DIL_SPAN = 128
BLOCK_Q_DIL = 128
NUM_BUCKETS = 32
MAX_DISTANCE = 2048
FOX_HEADS = 32
MLA_HEADS = 64
MLA_Q_RANK = 1536
MLA_KV_RANK = 512
MLA_NOPE_DIM = 128
MLA_ROPE_DIM = 64
MLA_V_DIM = 128
MLA_QK_DIM = MLA_NOPE_DIM + MLA_ROPE_DIM
QK_PAD = 2 * V7X_LANES
ROPE_THETA = 10000.0
D_FF = 11008
MASK_VALUE = -1e30
LOG2E = math.log2(math.e)
FLASH_TK = 512
N_SPLIT = 3
ROW_CHUNK = 256
CONV_HALO = 8


def _cparams(n_axes):
    return pltpu.CompilerParams(dimension_semantics=("arbitrary",) * n_axes,
                                vmem_limit_bytes=VMEM_LIMIT_BYTES)


def _tile(dim, pref):
    t = min(dim, pref)
    while dim % t:
        t //= 2
    return t


def _split_bf16(x):
    terms = []
    for _ in range(N_SPLIT - 1):
        t = x.astype(BF16)
        terms.append(t)
        x = x - t.astype(F32)
    terms.append(x.astype(BF16))
    return terms


def _rms_scale(x):
    return lax.rsqrt(jnp.mean(x * x, axis=-1, keepdims=True) + RMS_EPS)


def _rmsnorm_kernel(x_ref, g_ref, o_ref):
    x = x_ref[...].astype(F32)
    o_ref[...] = (x * _rms_scale(x) * g_ref[...]).astype(o_ref.dtype)


def rmsnorm(x, gain, *, width=None, col_block=0, rows=256):
    s, wtot = x.shape
    width = wtot if width is None else width
    tr = _tile(s, rows)
    return pl.pallas_call(
        _rmsnorm_kernel,
        grid=(s // tr,),
        in_specs=[pl.BlockSpec((tr, width), lambda i: (i, col_block)),
                  pl.BlockSpec((1, width), lambda i: (0, 0))],
        out_specs=pl.BlockSpec((tr, width), lambda i: (i, 0)),
        out_shape=jax.ShapeDtypeStruct((s, width), BF16),
        compiler_params=_cparams(1),
        name="rmsnorm",
    )(x, gain.reshape(1, width).astype(F32))


def _residue_perm(tr, d, to_residue_major):
    n = tr // d
    out_row = lax.broadcasted_iota(jnp.int32, (tr, tr), 0)
    in_row = lax.broadcasted_iota(jnp.int32, (tr, tr), 1)
    if to_residue_major:
        src = (out_row & (n - 1)) * d + (out_row >> (n.bit_length() - 1))
    else:
        src = (out_row & (d - 1)) * n + (out_row >> (d.bit_length() - 1))
    return jnp.where(in_row == src, 1.0, 0.0).astype(BF16)


def _rmsnorm_residue_kernel(x_ref, g_ref, *o_refs, dilations):
    tr = x_ref.shape[0]
    x = x_ref[...]
    hn = (x * _rms_scale(x) * g_ref[...]).astype(BF16)
    for o_ref, d in zip(o_refs, dilations):
        if d == 1:
            o_ref[0] = hn
            continue
        n = tr // d
        moved = jnp.dot(_residue_perm(tr, d, True), hn, preferred_element_type=F32).astype(BF16)
        for r in range(d):
            o_ref[r] = moved[r * n:(r + 1) * n, :]


def rmsnorm_residue_major(x, gain, dilations, *, rows=256):
    s, w = x.shape
    tr = _tile(s, rows)
    outs = pl.pallas_call(
        functools.partial(_rmsnorm_residue_kernel, dilations=dilations),
        grid=(s // tr,),
        in_specs=[pl.BlockSpec((tr, w), lambda i: (i, 0)),
                  pl.BlockSpec((1, w), lambda i: (0, 0))],
        out_specs=[pl.BlockSpec((d, tr // d, w), lambda i: (0, i, 0)) for d in dilations],
        out_shape=[jax.ShapeDtypeStruct((d, s // d, w), BF16) for d in dilations],
        compiler_params=_cparams(1),
        name="rmsnorm_residue",
    )(x, gain.reshape(1, w).astype(F32))
    return [o.reshape(s, w) for o in outs]


def _mm_kernel(*refs, nk, has_res):
    if has_res:
        a_ref, w_ref, r_ref, o_ref = refs[:4]
        scratch = refs[4:]
    else:
        a_ref, w_ref, o_ref = refs[:3]
        r_ref = None
        scratch = refs[3:]

    tm = a_ref.shape[0]
    rc = min(tm, ROW_CHUNK)
    chunks = [slice(c * rc, (c + 1) * rc) for c in range(tm // rc)]

    def part(rows):
        return jnp.dot(a_ref[rows, :], w_ref[...], preferred_element_type=F32)

    def finish(rows, acc):
        if has_res:
            acc = acc + r_ref[rows, :]
        o_ref[rows, :] = acc.astype(o_ref.dtype)

    def pipelined(epilogue):
        ahead = part(chunks[0])
        for c, rows in enumerate(chunks):
            cur = ahead
            if c + 1 < len(chunks):
                ahead = part(chunks[c + 1])
            epilogue(rows, cur)

    if nk == 1:
        pipelined(finish)
        return

    acc_ref = scratch[0]
    k = pl.program_id(2)

    def store(rows, p):
        acc_ref[rows, :] = p

    def accumulate(rows, p):
        acc_ref[rows, :] += p

    pl.when(k == 0)(lambda: pipelined(store))
    if nk > 2:
        pl.when((k > 0) & (k < nk - 1))(lambda: pipelined(accumulate))
    pl.when(k == nk - 1)(lambda: pipelined(lambda rows, p: finish(rows, acc_ref[rows, :] + p)))


def matmul(a, w, *, n=None, w_col0=0, residual=None, out_dtype=BF16, tm=1024, tn=1024, tk=4096):
    m, kdim = a.shape
    n = w.shape[1] if n is None else n
    tm, tn, tk = _tile(m, tm), _tile(n, tn), _tile(kdim, tk)
    nk = kdim // tk
    j0 = w_col0 // tn
    in_specs = [pl.BlockSpec((tm, tk), lambda i, j, k: (i, k)),
                pl.BlockSpec((tk, tn), lambda i, j, k: (k, j + j0))]
    args = [a, w]
    if residual is not None:
        in_specs.append(pl.BlockSpec((tm, tn), lambda i, j, k: (i, j)))
        args.append(residual)
    return pl.pallas_call(
        functools.partial(_mm_kernel, nk=nk, has_res=residual is not None),
        grid=(m // tm, n // tn, nk),
        in_specs=in_specs,
        out_specs=pl.BlockSpec((tm, tn), lambda i, j, k: (i, j)),
        out_shape=jax.ShapeDtypeStruct((m, n), out_dtype),
        scratch_shapes=[pltpu.VMEM((tm, tn), F32)] if nk > 1 else [],
        compiler_params=_cparams(3),
        name="matmul",
    )(*args)


def _t5_causal_bucket(dist):
    max_exact = NUM_BUCKETS // 2
    d32 = jnp.maximum(dist, 1).astype(F32)
    large = max_exact + (jnp.log(d32 / max_exact) / math.log(MAX_DISTANCE / max_exact)
                         * (NUM_BUCKETS - max_exact)).astype(jnp.int32)
    large = jnp.minimum(large, NUM_BUCKETS - 1)
    return jnp.where(dist < max_exact, dist, large)


def _dil_bucket_onehot():
    band = BLOCK_Q_DIL + DIL_SPAN
    rel = jnp.arange(BLOCK_Q_DIL)[:, None] + DIL_SPAN - jnp.arange(band)[None, :]
    valid = ((rel >= 0) & (rel <= DIL_SPAN)).astype(F32)
    relc = jnp.clip(rel, 0, DIL_SPAN)
    hots = []
    for window, dilation in DIL_CONFIGS:
        bucket = _t5_causal_bucket(relc.astype(jnp.int32) * dilation)
        hots.append((bucket[None] == jnp.arange(NUM_BUCKETS)[:, None, None]).astype(F32))
    return jnp.stack(hots), valid


def _dil_bias_kernel(rb_ref, oh_ref, valid_ref, o_ref):
    g = pl.program_id(0)
    valid = valid_ref[...] > 0
    for h in range(DIL_HEADS):
        def body(b, acc, h=h):
            return acc + rb_ref[b, g * DIL_HEADS + h] * oh_ref[0, b]
        acc = lax.fori_loop(0, NUM_BUCKETS, body, jnp.zeros(valid_ref.shape, F32))
        o_ref[0, h] = jnp.where(valid, acc, -jnp.inf)


def dil_bias_tiles(rel_bias):
    onehot, valid = _dil_bucket_onehot()
    ng = len(DIL_CONFIGS)
    band = BLOCK_Q_DIL + DIL_SPAN
    return pl.pallas_call(
        _dil_bias_kernel,
        grid=(ng,),
        in_specs=[pl.BlockSpec(memory_space=pltpu.SMEM),
                  pl.BlockSpec((1, NUM_BUCKETS, BLOCK_Q_DIL, band), lambda g: (g, 0, 0, 0)),
                  pl.BlockSpec((BLOCK_Q_DIL, band), lambda g: (0, 0))],
        out_specs=pl.BlockSpec((1, DIL_HEADS, BLOCK_Q_DIL, band), lambda g: (g, 0, 0, 0)),
        out_shape=jax.ShapeDtypeStruct((ng, DIL_HEADS, BLOCK_Q_DIL, band), F32),
        compiler_params=_cparams(1),
        name="dil_bias",
    )(rel_bias.astype(F32), onehot, valid)


def _dil_attn_kernel(q_ref, kp_ref, kc_ref, vp_ref, vc_ref, bias_ref, qg_ref, kg_ref, o_ref, l_ref):
    has_prev = pl.program_id(1) > 0
    tq = q_ref.shape[0]
    scale = HEAD_DIM ** -0.5
    qg = qg_ref[...] * scale
    kg = kg_ref[...]
    col = lax.broadcasted_iota(jnp.int32, (tq, 2 * tq), 1)
    keep = (col >= tq) | has_prev
    for h in range(DIL_HEADS):
        sl = slice(h * HEAD_DIM, (h + 1) * HEAD_DIM)
        q = q_ref[:, sl].astype(F32)
        qn = q * _rms_scale(q) * qg
        k = jnp.concatenate([kp_ref[:, sl], kc_ref[:, sl]], axis=0).astype(F32)
        kn = k * _rms_scale(k) * kg
        s = lax.dot_general(qn.astype(BF16), kn.astype(BF16), (((1,), (1,)), ((), ())),
                            preferred_element_type=F32)
        s = jnp.where(keep, s + bias_ref[h], -jnp.inf)
        m = jnp.max(s, axis=-1, keepdims=True)
        p = jnp.exp(s - m)
        l = jnp.sum(p, axis=-1, keepdims=True)
        v = jnp.concatenate([vp_ref[:, sl], vc_ref[:, sl]], axis=0)
        o = jnp.dot(p.astype(BF16), v, preferred_element_type=F32) / l
        o_ref[:, sl] = o.astype(o_ref.dtype)
        l_ref[:, sl] = jnp.broadcast_to(m + jnp.log(l), (tq, HEAD_DIM))


def dil_attention_group(qkv, bias_g, q_gain, k_gain, dilation):
    s = qkv.shape[0]
    hw = DIL_HEADS * HEAD_DIM
    tq = BLOCK_Q_DIL
    nb = s // dilation // tq

    def cur(c):
        return lambda r, i: (r * nb + i, c)

    def prev(c):
        return lambda r, i: (r * nb + jnp.maximum(i - 1, 0), c)

    blk = (tq, hw)
    return pl.pallas_call(
        _dil_attn_kernel,
        grid=(dilation, nb),
        in_specs=[pl.BlockSpec(blk, cur(0)),
                  pl.BlockSpec(blk, prev(1)), pl.BlockSpec(blk, cur(1)),
                  pl.BlockSpec(blk, prev(2)), pl.BlockSpec(blk, cur(2)),
                  pl.BlockSpec((DIL_HEADS, tq, 2 * tq), lambda r, i: (0, 0, 0)),
                  pl.BlockSpec((1, HEAD_DIM), lambda r, i: (0, 0)),
                  pl.BlockSpec((1, HEAD_DIM), lambda r, i: (0, 0))],
        out_specs=[pl.BlockSpec(blk, cur(0)), pl.BlockSpec(blk, cur(0))],
        out_shape=[jax.ShapeDtypeStruct((s, hw), F32), jax.ShapeDtypeStruct((s, hw), F32)],
        compiler_params=_cparams(2),
        name="dil_attn",
    )(qkv, qkv, qkv, qkv, qkv, bias_g,
      q_gain.reshape(1, HEAD_DIM).astype(F32), k_gain.reshape(1, HEAD_DIM).astype(F32))


def _dil_merge_kernel(*refs, dilations):
    ng = len(dilations)
    o_refs, l_refs, out_ref = refs[:ng], refs[ng:2 * ng], refs[2 * ng]
    tr, hw = out_ref.shape

    def token_order(ref, d):
        if d == 1:
            return ref[0]
        perm = _residue_perm(tr, d, False)
        x = jnp.concatenate([ref[r] for r in range(d)], axis=0)
        acc = jnp.zeros((tr, hw), F32)
        for term in _split_bf16(x):
            acc = acc + jnp.dot(perm, term, preferred_element_type=F32)
        return acc

    outs = [token_order(ref, d) for ref, d in zip(o_refs, dilations)]
    lses = [token_order(ref, d) for ref, d in zip(l_refs, dilations)]
    m = functools.reduce(jnp.maximum, lses)
    es = [jnp.exp(l - m) for l in lses]
    den = functools.reduce(jnp.add, es)
    out = functools.reduce(jnp.add, [(e / den) * o for e, o in zip(es, outs)])
    out_ref[...] = out.astype(out_ref.dtype)


def dil_merge(outs, lses, dilations):
    s, hw = outs[0].shape
    tr = _tile(s, 256)
    views = [a.reshape(d, s // d, hw) for a, d in zip(list(outs) + list(lses), list(dilations) * 2)]
    specs = [pl.BlockSpec((d, tr // d, hw), lambda i: (0, i, 0)) for d in list(dilations) * 2]
    return pl.pallas_call(
        functools.partial(_dil_merge_kernel, dilations=tuple(dilations)),
        grid=(s // tr,),
        in_specs=specs,
        out_specs=pl.BlockSpec((tr, hw), lambda i: (i, 0)),
        out_shape=jax.ShapeDtypeStruct((s, hw), BF16),
        compiler_params=_cparams(1),
        name="dil_merge",
    )(*views)


def dilated_mixer(x, bias_tiles, norm_g, w_in, q_norm, k_norm, w_out):
    dilations = tuple(d for _, d in DIL_CONFIGS)
    gw = 3 * DIL_HEADS * HEAD_DIM
    hs = rmsnorm_residue_major(x, norm_g, dilations)
    w_in = w_in.astype(BF16)
    outs, lses = [], []
    for g, (h, dilation) in enumerate(zip(hs, dilations)):
        qkv = matmul(h, w_in, n=gw, w_col0=g * gw)
        o, lse = dil_attention_group(qkv, bias_tiles[g], q_norm, k_norm, dilation)
        outs.append(o)
        lses.append(lse)
    merged = dil_merge(outs, lses, dilations)
    return matmul(merged, w_out.astype(BF16), residual=x, out_dtype=F32)


def _rotate_half_tile(y, c_ref, s1_ref, s2_ref):
    return y * c_ref[...] + pltpu.roll(y, 96, 1) * s1_ref[...] + pltpu.roll(y, 32, 1) * s2_ref[...]


def _head_prep_kernel(*refs, pieces, n_src, hb, dn, scale, rope, extra):
    g_ref = refs[n_src]
    nxt = n_src + 1
    if rope:
        tabs = refs[nxt:nxt + 3]
        nxt += 3
    if extra == "gate":
        cl_ref = refs[nxt]
        nxt += 1
    o_ref = refs[nxt]
    npieces = len(pieces)
    w_out = o_ref.shape[1] // hb
    tr = o_ref.shape[0]
    lane = lax.broadcasted_iota(jnp.int32, (tr, V7X_LANES), 1)
    for hh in range(hb):
        xs = []
        for si, wph, off in pieces:
            lo = off if wph is None else hh * wph + off
            xs.append(refs[si][:, lo:lo + V7X_LANES].astype(F32))
        ssq = jnp.sum(xs[0] * xs[0], axis=-1, keepdims=True)
        for x in xs[1:]:
            ssq = ssq + jnp.sum(x * x, axis=-1, keepdims=True)
        r = lax.rsqrt(ssq * (1.0 / dn) + RMS_EPS) * scale
        base = hh * w_out
        for p, x in enumerate(xs):
            y = x * r * g_ref[:, p * V7X_LANES:(p + 1) * V7X_LANES]
            if rope and p == npieces - 1:
                y = _rotate_half_tile(y, *tabs)
            o_ref[:, base + p * V7X_LANES:base + (p + 1) * V7X_LANES] = y.astype(o_ref.dtype)
        tail = slice(base + npieces * V7X_LANES, base + (npieces + 1) * V7X_LANES)
        if extra == "ones":
            o_ref[:, tail] = jnp.where(lane < N_SPLIT, 1.0, 0.0).astype(o_ref.dtype)
        elif extra == "gate":
            head = pl.program_id(1) * hb + hh
            col = jnp.sum(jnp.where(lane == head, cl_ref[...], 0.0), axis=-1, keepdims=True)
            d = col[0:1, :] - col
            y = jnp.zeros((tr, V7X_LANES), F32)
            for n, term in enumerate(_split_bf16(d)):
                y = jnp.where(lane == n, term.astype(F32), y)
            o_ref[:, tail] = y.astype(o_ref.dtype)


def head_prep(pieces, gain, n_heads, *, dn, scale=1.0, rope_tables=None, extra=None, gate_cum=None,
              rows=1024, hb=4):
    s = pieces[0][0].shape[0]
    tr = _tile(s, rows)
    srcs, in_specs, args, kpieces = {}, [], [], []
    for arr, wph, off, col0 in pieces:
        key = (id(arr), wph, col0 if wph is not None else off)
        if key not in srcs:
            srcs[key] = len(args)
            args.append(arr)
            if wph is None:
                in_specs.append(pl.BlockSpec((tr, V7X_LANES), lambda i, h, c=off // V7X_LANES: (i, c)))
            else:
                in_specs.append(pl.BlockSpec((tr, hb * wph), lambda i, h, c=col0 // (hb * wph): (i, c + h)))
        kpieces.append((srcs[key], wph, 0 if wph is None else off))
    n_src = len(args)
    npieces = len(pieces)
    w_in = npieces * V7X_LANES
    w_out = w_in + (V7X_LANES if extra else 0)
    in_specs.append(pl.BlockSpec((1, w_in), lambda i, h: (0, 0)))
    args.append(gain.reshape(1, w_in).astype(F32))
    shared = pl.BlockSpec((tr, V7X_LANES), lambda i, h: (i, 0))
    if rope_tables is not None:
        in_specs += [shared] * 3
        args += list(rope_tables)
    if extra == "gate":
        in_specs.append(shared)
        args.append(gate_cum)
    return pl.pallas_call(
        functools.partial(_head_prep_kernel, pieces=tuple(kpieces), n_src=n_src, hb=hb, dn=dn, scale=scale,
                          rope=rope_tables is not None, extra=extra),
        grid=(s // tr, n_heads // hb),
        in_specs=in_specs,
        out_specs=pl.BlockSpec((tr, hb * w_out), lambda i, h: (i, h)),
        out_shape=jax.ShapeDtypeStruct((s, n_heads * w_out), BF16),
        compiler_params=_cparams(2),
        name="head_prep",
    )(*args)


ONES_ROWS = 16


def _flash_kernel(*refs, tq, tk, dv, has_g, n_chains, depth):
    if has_g:
        g_ref, q_ref, k_ref, v_ref, o_ref, acc_ref, s_ref = refs
    else:
        q_ref, k_ref, v_ref, o_ref, acc_ref, s_ref = refs
        g_ref = None
    h = pl.program_id(0)
    i = pl.program_id(1)
    ratio = tq // tk
    cw = tq // n_chains
    acc_ref[...] = jnp.zeros(acc_ref.shape, F32)
    ones = jnp.ones((ONES_ROWS, tk), BF16)

    def live_chains(diag):
        return [c for c in range(n_chains) if diag is None or (c + 1) * cw > diag * tk]

    def k_block(j):
        return k_ref[pl.ds(pl.multiple_of(j * tk, tk), tk), :]

    def qk(kb, c):
        return lax.dot_general(kb, q_ref[c * cw:(c + 1) * cw, :], (((1,), (1,)), ((), ())),
                               preferred_element_type=F32)

    def step(j, m_old, diag, next_diag):
        kb = k_block(j)
        vtb = jnp.concatenate([v_ref[pl.ds(pl.multiple_of(j * tk, tk), tk), :].T, ones], axis=0)
        if has_g:
            g = g_ref[h, i * ratio] - g_ref[h, j]
        rd = (j % 2) * depth
        wr = depth - rd
        live = live_chains(diag)
        nxt = []
        if next_diag is not None:
            nxt = live_chains(None if next_diag == "full" else next_diag)[:depth]
            kb_next = k_block(j + 1)

        def logits(n):
            c = live[n]
            s_t = s_ref[rd + n] if n < depth else qk(kb, c)
            if diag is not None and c * cw < (diag + 1) * tk:
                key = lax.broadcasted_iota(jnp.int32, (tk, cw), 0) + diag * tk
                qry = lax.broadcasted_iota(jnp.int32, (tk, cw), 1) + c * cw
                s_t = jnp.where(key <= qry, s_t, MASK_VALUE)
            return s_t

        jobs = [("own", n) for n in range(depth, len(live))] + [("next", n) for n in range(len(nxt))]
        pending = {}

        def issue():
            if jobs:
                kind, n = jobs.pop(0)
                if kind == "own":
                    pending[n] = logits(n)
                else:
                    s_ref[wr + n] = qk(kb_next, nxt[n])

        for _ in range(depth):
            issue()
        m_parts = {}
        for n, c in enumerate(live):
            cs = slice(c * cw, (c + 1) * cw)
            s_t = logits(n) if n < depth else pending.pop(n)
            mb = jnp.max(s_t, axis=0, keepdims=True)
            if has_g:
                mb = mb + g
            mo = m_old[:, cs]
            mn = jnp.maximum(mo, mb)
            alpha = jnp.exp2(mo - mn)
            sub = (mn - g) if has_g else mn
            p_t = jnp.exp2(s_t - sub).astype(BF16)
            issue()
            acc_ref[:, cs] = alpha * acc_ref[:, cs] + jnp.dot(vtb, p_t, preferred_element_type=F32)
            m_parts[c] = mn
        while jobs:
            issue()
        parts = [m_parts[c] if c in m_parts else m_old[:, c * cw:(c + 1) * cw] for c in range(n_chains)]
        return jnp.concatenate(parts, axis=1) if n_chains > 1 else parts[0]

    kb0 = k_block(0)
    for slot in range(depth):
        s_ref[slot] = qk(kb0, slot)
    m = jnp.full((1, tq), MASK_VALUE, F32)
    unroll = 2 if ratio % 2 == 0 else 1

    def full_steps(jj, mm):
        for u in range(unroll):
            mm = step(jj * unroll + u, mm, None, "full")
        return mm

    m = lax.fori_loop(0, (i * ratio) // unroll, full_steps, m)
    for d in range(ratio):
        m = step(i * ratio + d, m, d, d + 1 if d + 1 < ratio else None)
    acc = acc_ref[...]
    o_t = acc[:dv, :] / acc[dv:dv + 1, :]
    o_ref[...] = o_t.T.astype(o_ref.dtype)


def flash_attention(q, k, v, n_heads, *, dv, v_col, block_offsets=None, tq=2048, depth=2):
    s = q.shape[0]
    tk = _tile(s, FLASH_TK)
    tq = max(_tile(s, tq), tk)
    n_chains = max(tq // V7X_MXU_DIM, depth)
    in_specs = [pl.BlockSpec((tq, QK_PAD), lambda h, i: (i, h)),
                pl.BlockSpec((s, QK_PAD), lambda h, i: (0, h)),
                pl.BlockSpec((s, dv), lambda h, i: (0, v_col(h)))]
    args = [q, k, v]
    if block_offsets is not None:
        in_specs.insert(0, pl.BlockSpec(memory_space=pltpu.SMEM))
        args.insert(0, block_offsets)
    return pl.pallas_call(
        functools.partial(_flash_kernel, tq=tq, tk=tk, dv=dv, has_g=block_offsets is not None,
                          n_chains=n_chains, depth=depth),
        grid=(n_heads, s // tq),
        in_specs=in_specs,
        out_specs=pl.BlockSpec((tq, dv), lambda h, i: (i, h)),
        out_shape=jax.ShapeDtypeStruct((s, n_heads * dv), BF16),
        scratch_shapes=[pltpu.VMEM((dv + ONES_ROWS, tq), F32),
                        pltpu.VMEM((2 * depth, tk, tq // n_chains), F32)],
        compiler_params=_cparams(2),
        name="flash_attn",
    )(*args)


def _fox_gate_kernel(h_ref, wg_ref, bf_ref, o_ref, carry_ref):
    i = pl.program_id(0)
    ts = h_ref.shape[0]

    @pl.when(i == 0)
    def _():
        carry_ref[...] = jnp.zeros_like(carry_ref)

    z = jnp.dot(h_ref[...], wg_ref[...], preferred_element_type=F32) + bf_ref[...]
    lf = jnp.minimum(z, 0.0) - jnp.log(1.0 + jnp.exp(-jnp.abs(z)))
    row = lax.broadcasted_iota(jnp.int32, (ts, ts), 0)
    colm = lax.broadcasted_iota(jnp.int32, (ts, ts), 1)
    lower = jnp.where(row >= colm, 1.0, 0.0).astype(BF16)
    cs = carry_ref[0:1, :]
    for term in _split_bf16(lf):
        cs = cs + jnp.dot(lower, term, preferred_element_type=F32)
    carry_ref[...] = jnp.broadcast_to(cs[ts - 1:ts, :], carry_ref.shape)
    o_ref[...] = cs * LOG2E


def fox_cum_log_gate(h, w_gate, b_f):
    s, d = h.shape
    ts = _tile(s, 512)
    return pl.pallas_call(
        _fox_gate_kernel,
        grid=(s // ts,),
        in_specs=[pl.BlockSpec((ts, d), lambda i: (i, 0)),
                  pl.BlockSpec((d, V7X_LANES), lambda i: (0, 0)),
                  pl.BlockSpec((1, V7X_LANES), lambda i: (0, 0))],
        out_specs=pl.BlockSpec((ts, V7X_LANES), lambda i: (i, 0)),
        out_shape=jax.ShapeDtypeStruct((s, V7X_LANES), F32),
        scratch_shapes=[pltpu.VMEM((8, V7X_LANES), F32)],
        compiler_params=_cparams(1),
        name="fox_gate",
    )(h, w_gate, b_f)


def fox_mixer(x, norm_g, w_in, b_f, q_norm, k_norm, w_out):
    s = x.shape[0]
    width = FOX_HEADS * HEAD_DIM
    tk = _tile(s, FLASH_TK)
    h = rmsnorm(x, norm_g)
    w_in = w_in.astype(BF16)
    proj = matmul(h, w_in, n=3 * width)
    w_gate = jnp.pad(w_in[:, 3 * width:], ((0, 0), (0, V7X_LANES - FOX_HEADS)))
    b_gate = jnp.pad(b_f.astype(F32), (0, V7X_LANES - FOX_HEADS)).reshape(1, V7X_LANES)
    cl = fox_cum_log_gate(h, w_gate, b_gate)
    q = head_prep([(proj, HEAD_DIM, 0, 0)], q_norm, FOX_HEADS, dn=HEAD_DIM,
                  scale=HEAD_DIM ** -0.5 * LOG2E, extra="ones")
    k = head_prep([(proj, HEAD_DIM, 0, width)], k_norm, FOX_HEADS, dn=HEAD_DIM,
                  extra="gate", gate_cum=cl, rows=tk, hb=8)
    offsets = cl[::tk, :FOX_HEADS].T
    o = flash_attention(q, k, proj, FOX_HEADS, dv=HEAD_DIM, v_col=lambda hh: 2 * FOX_HEADS + hh,
                        block_offsets=offsets)
    return matmul(o, w_out.astype(BF16), residual=x, out_dtype=F32, tk=2048)


def _rope_table_kernel(pos_ref, freq_ref, c_ref, s1_ref, s2_ref):
    ang = pos_ref[...].astype(F32) * freq_ref[...]
    lane = lax.broadcasted_iota(jnp.int32, ang.shape, 1)
    half = MLA_ROPE_DIM // 2
    c = jnp.cos(ang)
    sn = jnp.sin(ang)
    c_ref[...] = jnp.where(lane < 2 * half, c, 0.0)
    s1_ref[...] = jnp.where(lane < half, -sn, 0.0)
    s2_ref[...] = jnp.where((lane >= half) & (lane < 2 * half), sn, 0.0)


def rope_tables(positions):
    s = positions.shape[0]
    half = MLA_ROPE_DIM // 2
    freqs = ROPE_THETA ** (-jnp.arange(half, dtype=F32) / half)
    freq_row = jnp.concatenate([freqs, freqs, jnp.zeros((V7X_LANES - 2 * half,), F32)]).reshape(1, V7X_LANES)
    tr = _tile(s, 1024)
    spec = pl.BlockSpec((tr, V7X_LANES), lambda i: (i, 0))
    return pl.pallas_call(
        _rope_table_kernel,
        grid=(s // tr,),
        in_specs=[pl.BlockSpec((tr, 1), lambda i: (i, 0)), pl.BlockSpec((1, V7X_LANES), lambda i: (0, 0))],
        out_specs=[spec] * 3,
        out_shape=[jax.ShapeDtypeStruct((s, V7X_LANES), F32)] * 3,
        compiler_params=_cparams(1),
        name="rope_tables",
    )(positions.reshape(s, 1), freq_row)


def _pad_cols(w, to):
    return jnp.pad(w, ((0, 0), (0, to - w.shape[1])))


def mla_mixer(x, positions, norm_g, w_in, q_a_norm, kv_a_norm, w_q_b, w_kv_b, q_norm, k_norm, w_out):
    lat_w = MLA_Q_RANK + MLA_KV_RANK
    in_w = 2304
    h = rmsnorm(x, norm_g)
    proj = matmul(h, _pad_cols(w_in, in_w).astype(BF16), out_dtype=F32, tn=1152)
    q_lat = rmsnorm(proj, q_a_norm, width=MLA_Q_RANK, col_block=0)
    kv_lat = rmsnorm(proj, kv_a_norm, width=MLA_KV_RANK, col_block=MLA_Q_RANK // MLA_KV_RANK)
    wq = jnp.pad(w_q_b.reshape(MLA_Q_RANK, MLA_HEADS, MLA_QK_DIM), ((0, 0), (0, 0), (0, QK_PAD - MLA_QK_DIM)))
    q_raw = matmul(q_lat, wq.reshape(MLA_Q_RANK, MLA_HEADS * QK_PAD).astype(BF16))
    kv_raw = matmul(kv_lat, w_kv_b.astype(BF16))
    tabs = rope_tables(positions)
    qg = jnp.pad(q_norm, (0, QK_PAD - MLA_QK_DIM))
    kg = jnp.pad(k_norm, (0, QK_PAD - MLA_QK_DIM))
    q = head_prep([(q_raw, QK_PAD, 0, 0), (q_raw, QK_PAD, V7X_LANES, 0)], qg, MLA_HEADS,
                  dn=MLA_QK_DIM, scale=MLA_QK_DIM ** -0.5 * LOG2E, rope_tables=tabs)
    k = head_prep([(kv_raw, 2 * V7X_LANES, 0, 0), (proj, None, lat_w, None)], kg, MLA_HEADS,
                  dn=MLA_QK_DIM, rope_tables=tabs)
    o = flash_attention(q, k, kv_raw, MLA_HEADS, dv=MLA_V_DIM, v_col=lambda hh: 2 * hh + 1)
    return matmul(o, w_out.astype(BF16), residual=x, out_dtype=F32, tk=2048)


def _ffn_up_kernel(a_ref, wg_ref, wv_ref, cwg_ref, cwv_ref, cbg_ref, cbv_ref, o_ref, carry_ref, u_ref):
    i = pl.program_id(0)
    j = pl.program_id(1)
    tm = a_ref.shape[0]
    rc = u_ref.shape[2] - CONV_HALO
    nchunks = tm // rc

    @pl.when(i == 0)
    def _():
        carry_ref[j] = jnp.zeros(carry_ref.shape[1:], F32)

    def dots(c):
        a = a_ref[c * rc:(c + 1) * rc, :]
        u_ref[c % 2, 0, CONV_HALO:, :] = jnp.dot(a, wg_ref[...], preferred_element_type=F32)
        u_ref[c % 2, 1, CONV_HALO:, :] = jnp.dot(a, wv_ref[...], preferred_element_type=F32)

    def conv(slot, which, cw_ref, cb_ref):
        u0 = u_ref[slot, which, CONV_HALO:, :]
        u1 = u_ref[slot, which, CONV_HALO - 1:CONV_HALO - 1 + rc, :]
        u2 = u_ref[slot, which, CONV_HALO - 2:CONV_HALO - 2 + rc, :]
        return cw_ref[2:3, :] * u0 + cw_ref[1:2, :] * u1 + cw_ref[0:1, :] * u2 + cb_ref[...]

    u_ref[0, :, :CONV_HALO, :] = carry_ref[j]
    dots(0)
    for c in range(nchunks):
        slot = c % 2
        if c + 1 < nchunks:
            dots(c + 1)
        gate = conv(slot, 0, cwg_ref, cbg_ref)
        val = conv(slot, 1, cwv_ref, cbv_ref)
        o_ref[c * rc:(c + 1) * rc, :] = (gate * (1.0 / (1.0 + jnp.exp(-gate))) * val).astype(o_ref.dtype)
        tail = u_ref[slot, :, rc:, :]
        if c + 1 < nchunks:
            u_ref[1 - slot, :, :CONV_HALO, :] = tail
        else:
            carry_ref[j] = tail


def ffn_up(h, w_up, conv_w, conv_b, *, tm=2048, tn=256):
    s, d = h.shape
    f = w_up.shape[1] // 2
    tm = _tile(s, tm)
    nj = f // tn
    cw = conv_w.astype(F32)
    cb = conv_b.reshape(1, 2 * f).astype(F32)
    return pl.pallas_call(
        _ffn_up_kernel,
        grid=(s // tm, nj),
        in_specs=[pl.BlockSpec((tm, d), lambda i, j: (i, 0)),
                  pl.BlockSpec((d, tn), lambda i, j: (0, j)),
                  pl.BlockSpec((d, tn), lambda i, j: (0, j + nj)),
                  pl.BlockSpec((3, tn), lambda i, j: (0, j)),
                  pl.BlockSpec((3, tn), lambda i, j: (0, j + nj)),
                  pl.BlockSpec((1, tn), lambda i, j: (0, j)),
                  pl.BlockSpec((1, tn), lambda i, j: (0, j + nj))],
        out_specs=pl.BlockSpec((tm, tn), lambda i, j: (i, j)),
        out_shape=jax.ShapeDtypeStruct((s, f), BF16),
        scratch_shapes=[pltpu.VMEM((nj, 2, CONV_HALO, tn), F32),
                        pltpu.VMEM((2, 2, CONV_HALO + min(tm, ROW_CHUNK), tn), F32)],
        compiler_params=_cparams(2),
        name="ffn_up",
    )(h, w_up, w_up, cw, cw, cb, cb)


def conv_ffn(x, norm_g, w_up, conv_w, conv_b, w_down):
    h = rmsnorm(x, norm_g)
    y = ffn_up(h, w_up.astype(BF16), conv_w, conv_b)
    return matmul(y, w_down.astype(BF16), residual=x, out_dtype=F32, tn=512, tk=w_down.shape[0] // 2)


def kernel(x, positions, rel_bias,
           l0_attn_norm, l0_w_in, l0_q_norm, l0_k_norm, l0_w_out,
           l0_ffn_norm, l0_w_up, l0_conv_w, l0_conv_b, l0_w_down,
           l1_attn_norm, l1_w_in, l1_b_f, l1_q_norm, l1_k_norm, l1_w_out,
           l1_ffn_norm, l1_w_up, l1_conv_w, l1_conv_b, l1_w_down,
           l2_attn_norm, l2_w_in, l2_q_a_norm, l2_kv_a_norm, l2_w_q_b, l2_w_kv_b, l2_q_norm, l2_k_norm, l2_w_out,
           l2_ffn_norm, l2_w_up, l2_conv_w, l2_conv_b, l2_w_down,
           l3_attn_norm, l3_w_in, l3_q_norm, l3_k_norm, l3_w_out,
           l3_ffn_norm, l3_w_up, l3_conv_w, l3_conv_b, l3_w_down):
    b, s, d = x.shape
    outs = []
    bias_tiles = dil_bias_tiles(rel_bias)
    for bi in range(b):
        xs = x[bi]
        xs = dilated_mixer(xs, bias_tiles, l0_attn_norm, l0_w_in, l0_q_norm, l0_k_norm, l0_w_out)
        xs = conv_ffn(xs, l0_ffn_norm, l0_w_up, l0_conv_w, l0_conv_b, l0_w_down)
        xs = fox_mixer(xs, l1_attn_norm, l1_w_in, l1_b_f, l1_q_norm, l1_k_norm, l1_w_out)
        xs = conv_ffn(xs, l1_ffn_norm, l1_w_up, l1_conv_w, l1_conv_b, l1_w_down)
        xs = mla_mixer(xs, positions[bi], l2_attn_norm, l2_w_in, l2_q_a_norm, l2_kv_a_norm,
                       l2_w_q_b, l2_w_kv_b, l2_q_norm, l2_k_norm, l2_w_out)
        xs = conv_ffn(xs, l2_ffn_norm, l2_w_up, l2_conv_w, l2_conv_b, l2_w_down)
        xs = dilated_mixer(xs, bias_tiles, l3_attn_norm, l3_w_in, l3_q_norm, l3_k_norm, l3_w_out)
        xs = conv_ffn(xs, l3_ffn_norm, l3_w_up, l3_conv_w, l3_conv_b, l3_w_down)
        outs.append(xs)
    return jnp.stack(outs)
```
